```python
import jax
import jax.numpy as jnp
from jax import lax
import numpy as np

D_MODEL = 4096
BATCH = 2
SEQ = 8192
DEPTH = 1

GLA_HEADS = 4
GLA_DK = D_MODEL // 16
GLA_DV = D_MODEL // 8
GLA_KEY = GLA_HEADS * GLA_DK
GLA_VAL = GLA_HEADS * GLA_DV
GLA_RANK = 16
GLA_TAU = 16.0
GLA_CHUNK = 64
SGU_WIDTH = D_MODEL // 2
SGU_GROUPS = 8
SGU_GROUP_DIM = SGU_WIDTH // SGU_GROUPS
SGU_CHUNK = 128
N_EXPERTS = 32
TOP_K = 4
D_EXPERT = D_MODEL // 4
SWIGLU_ALPHA = 1.702
SWIGLU_LIMIT = 7.0
MOE_BLOCK = 128
N_ADA = 6
EPS = 1e-5
IN_SIZES = (GLA_KEY, GLA_KEY, GLA_VAL, GLA_VAL, GLA_RANK, SGU_WIDTH, SGU_WIDTH, D_MODEL, D_MODEL)
D_IN = GLA_KEY * 2 + GLA_VAL * 2 + GLA_RANK + SGU_WIDTH * 2 + D_MODEL * 2

kernel_name = 'hybrid_gla_sgu_moe_adaln'


def _rmsnorm(x, g):
    xf = x.astype(jnp.float32)
    inv = lax.rsqrt(jnp.mean(xf * xf, axis=-1, keepdims=True) + EPS)
    return (xf * inv).astype(x.dtype) * g


def _layernorm(x, g, b):
    xf = x.astype(jnp.float32)
    mu = jnp.mean(xf, axis=-1, keepdims=True)
    var = jnp.mean(jnp.square(xf - mu), axis=-1, keepdims=True)
    return ((xf - mu) * lax.rsqrt(var + EPS)).astype(x.dtype) * g + b


def _split(t, sizes):
    outs, start = [], 0
    for s in sizes:
        outs.append(t[..., start:start + s])
        start += s
    return outs


def _gla(q, k, v, log_a):
    b_, s_ = q.shape[0], q.shape[1]
    n = s_ // GLA_CHUNK

    def chunked(t):
        return t.reshape(b_, n, GLA_CHUNK, GLA_HEADS, -1).transpose(1, 0, 3, 2, 4).astype(jnp.float32)

    q, k, v, log_a = chunked(q), chunked(k), chunked(v), chunked(log_a)
    q = q * (GLA_DK ** -0.5)
    cum = jnp.cumsum(log_a, axis=3)
    cum_last = cum[:, :, :, -1:, :]
    q_abs = q * jnp.exp(cum)
    q_rel = q * jnp.exp(cum - cum_last)
    k_rel = k * jnp.exp(cum_last - cum)
    causal = jnp.tril(jnp.ones((GLA_CHUNK, GLA_CHUNK), dtype=bool))
    scores = jnp.einsum('nbhik,nbhjk->nbhij', q_rel, k_rel)
    scores = jnp.where(causal, scores, 0.0)
    o_intra = jnp.einsum('nbhij,nbhjv->nbhiv', scores, v)
    chunk_decay = jnp.exp(cum_last[:, :, :, 0, :])

    def step(state, inp):
        qa, kr, vc, dec = inp
        o = jnp.einsum('bhik,bhkv->bhiv', qa, state)
        state = dec[..., None] * state + jnp.einsum('bhjk,bhjv->bhkv', kr, vc)
        return state, o

    state0 = jnp.zeros((b_, GLA_HEADS, GLA_DK, GLA_DV), jnp.float32)
    _, o_inter = lax.scan(step, state0, (q_abs, k_rel, v, chunk_decay))
    o = o_intra + o_inter
    return o.transpose(1, 0, 3, 2, 4).reshape(b_, s_, GLA_HEADS, GLA_DV)


def _sgu(u, z, ln_g, ln_b, w_spatial, b_spatial):
    b_, s_ = u.shape[0], u.shape[1]
    n = s_ // SGU_CHUNK
    u = jax.nn.gelu(u)
    z = _layernorm(jax.nn.gelu(z), ln_g, ln_b)
    zc = z.reshape(b_, n, SGU_CHUNK, SGU_GROUPS, SGU_GROUP_DIM)
    mask = jnp.tril(jnp.ones((SGU_CHUNK, SGU_CHUNK), dtype=w_spatial.dtype))
    w = w_spatial * mask
    s = jnp.einsum('gts,bnsgc->bntgc', w, zc) + b_spatial.T[None, None, :, :, None]
    return u * s.reshape(b_, s_, SGU_WIDTH)


def _mixer(h, w_in, w_alpha_up, b_alpha, gla_norm_g, sgu_ln_g, sgu_ln_b, w_spatial, b_spatial,
           w_branch_a, w_branch_b, w_out):
    b_, s_ = h.shape[0], h.shape[1]
    proj = h @ w_in
    q, k, v, g, a_lr, u, z, gate_a, gate_b = _split(proj, IN_SIZES)
    log_a = jax.nn.log_sigmoid((a_lr @ w_alpha_up + b_alpha).astype(jnp.float32)) / GLA_TAU
    heads_k = lambda t: t.reshape(b_, s_, GLA_HEADS, GLA_DK)
    o = _gla(heads_k(q), heads_k(k), v.reshape(b_, s_, GLA_HEADS, GLA_DV), heads_k(log_a))
    o = _rmsnorm(o.astype(h.dtype), gla_norm_g).reshape(b_, s_, GLA_VAL)
    y_a = (o * jax.nn.silu(g)) @ w_branch_a
    y_b = _sgu(u, z, sgu_ln_g, sgu_ln_b, w_spatial, b_spatial) @ w_branch_b
    merged = jax.nn.sigmoid(gate_a) * y_a + jax.nn.sigmoid(gate_b) * y_b
    return merged @ w_out


def _moe(h, w_router, b_router, w_exp_gate, b_exp_gate, w_exp_up, b_exp_up, w_exp_down, b_exp_down):
    b_, s_, d = h.shape
    t = b_ * s_
    xt = h.reshape(t, d)
    logits = xt.astype(jnp.float32) @ w_router.astype(jnp.float32) + b_router.astype(jnp.float32)
    top_val, top_idx = lax.top_k(logits, TOP_K)
    top_w = jax.nn.softmax(top_val, axis=-1)
    n_assign = t * TOP_K
    e_flat = top_idx.reshape(n_assign).astype(jnp.int32)
    tok_flat = jnp.repeat(jnp.arange(t, dtype=jnp.int32), TOP_K)
    w_flat = top_w.reshape(n_assign)
    order = jnp.argsort(e_flat)
    e_sorted, tok_sorted, w_sorted = e_flat[order], tok_flat[order], w_flat[order]
    counts = jnp.zeros((N_EXPERTS,), jnp.int32).at[e_flat].add(1)
    starts = jnp.cumsum(counts) - counts
    padded = (counts + MOE_BLOCK - 1) // MOE_BLOCK * MOE_BLOCK
    padded_end = jnp.cumsum(padded)
    padded_start = padded_end - padded
    dest = padded_start[e_sorted] + (jnp.arange(n_assign, dtype=jnp.int32) - starts[e_sorted])
    n_rows = n_assign + N_EXPERTS * MOE_BLOCK
    n_blocks = n_rows // MOE_BLOCK
    row_tok = jnp.zeros((n_rows,), jnp.int32).at[dest].set(tok_sorted)
    row_w = jnp.zeros((n_rows,), jnp.float32).at[dest].set(w_sorted)
    block_start = jnp.arange(n_blocks, dtype=jnp.int32) * MOE_BLOCK
    block_exp = jnp.minimum(jnp.sum(block_start[:, None] >= padded_end[None, :], axis=1), N_EXPERTS - 1)

    def body(y, inp):
        e, tok, wt = inp
        xb = xt[tok]
        x_glu = jnp.minimum(xb @ w_exp_gate[e] + b_exp_gate[e], SWIGLU_LIMIT)
        x_lin = jnp.clip(xb @ w_exp_up[e] + b_exp_up[e], -SWIGLU_LIMIT, SWIGLU_LIMIT)
        act = x_glu * jax.nn.sigmoid(SWIGLU_ALPHA * x_glu) * (x_lin + 1.0)
        out = act @ w_exp_down[e] + b_exp_down[e]
        y = y.at[tok].add(out * wt[:, None].astype(out.dtype))
        return y, None

    y, _ = lax.scan(body, jnp.zeros_like(xt),
                    (block_exp, row_tok.reshape(n_blocks, MOE_BLOCK), row_w.reshape(n_blocks, MOE_BLOCK)))
    return y.reshape(b_, s_, d)


def setup_inputs(seed: int = 0) -> dict:
    key = jax.random.key(seed)
    ks = jax.random.split(key, 32)
    f32 = jnp.float32
    nrm = lambda k, shape, scale: jax.random.normal(k, shape, f32) * scale
    gain = lambda k, shape: 1.0 + 0.02 * jax.random.normal(k, shape, f32)
    L = DEPTH
    return {
        'x': nrm(ks[0], (BATCH, SEQ, D_MODEL), 1.0),
        'c': nrm(ks[1], (BATCH, D_MODEL), 1.0),
        'w_ada': nrm(ks[2], (L, D_MODEL, N_ADA * D_MODEL), 0.2 * D_MODEL ** -0.5),
        'b_ada': nrm(ks[3], (L, N_ADA * D_MODEL), 0.01),
        'norm_mix_g': gain(ks[4], (L, D_MODEL)),
        'w_in': nrm(ks[5], (L, D_MODEL, D_IN), D_MODEL ** -0.5),
        'w_alpha_up': nrm(ks[6], (L, GLA_RANK, GLA_KEY), GLA_RANK ** -0.5),
        'b_alpha': nrm(ks[7], (L, GLA_KEY), 0.1),
        'gla_norm_g': gain(ks[8], (L, GLA_DV)),
        'sgu_ln_g': gain(ks[9], (L, SGU_WIDTH)),
        'sgu_ln_b': nrm(ks[10], (L, SGU_WIDTH), 0.01),
        'w_spatial': nrm(ks[11], (L, SGU_GROUPS, SGU_CHUNK, SGU_CHUNK), SGU_CHUNK ** -0.5),
        'b_spatial': gain(ks[12], (L, SGU_GROUPS, SGU_CHUNK)),
        'w_branch_a': nrm(ks[13], (L, GLA_VAL, D_MODEL), GLA_VAL ** -0.5),
        'w_branch_b': nrm(ks[14], (L, SGU_WIDTH, D_MODEL), SGU_WIDTH ** -0.5),
        'w_out': nrm(ks[15], (L, D_MODEL, D_MODEL), D_MODEL ** -0.5),
        'norm_ffn_g': gain(ks[16], (L, D_MODEL)),
        'w_router': nrm(ks[17], (L, D_MODEL, N_EXPERTS), D_MODEL ** -0.5),
        'b_router': nrm(ks[18], (L, N_EXPERTS), 0.01),
        'w_exp_gate': nrm(ks[19], (L, N_EXPERTS, D_MODEL, D_EXPERT), D_MODEL ** -0.5),
        'b_exp_gate': nrm(ks[20], (L, N_EXPERTS, D_EXPERT), 0.01),
        'w_exp_up': nrm(ks[21], (L, N_EXPERTS, D_MODEL, D_EXPERT), D_MODEL ** -0.5),
        'b_exp_up': nrm(ks[22], (L, N_EXPERTS, D_EXPERT), 0.01),
        'w_exp_down': nrm(ks[23], (L, N_EXPERTS, D_EXPERT, D_MODEL), D_EXPERT ** -0.5),
        'b_exp_down': nrm(ks[24], (L, N_EXPERTS, D_MODEL), 0.01),
        'norm_final_g': gain(ks[25], (D_MODEL,)),
    }


def reference(x, c, w_ada, b_ada, norm_mix_g, w_in, w_alpha_up, b_alpha, gla_norm_g, sgu_ln_g, sgu_ln_b,
              w_spatial, b_spatial, w_branch_a, w_branch_b, w_out, norm_ffn_g, w_router, b_router,
              w_exp_gate, b_exp_gate, w_exp_up, b_exp_up, w_exp_down, b_exp_down, norm_final_g):
    for l in range(DEPTH):
        mod = jax.nn.silu(c) @ w_ada[l] + b_ada[l]
        sh1, sc1, g1, sh2, sc2, g2 = jnp.split(mod, N_ADA, axis=-1)
        h = _rmsnorm(x, norm_mix_g[l]) * (1.0 + sc1[:, None, :]) + sh1[:, None, :]
        x = x + g1[:, None, :] * _mixer(h, w_in[l], w_alpha_up[l], b_alpha[l], gla_norm_g[l], sgu_ln_g[l],
                                        sgu_ln_b[l], w_spatial[l], b_spatial[l], w_branch_a[l],
                                        w_branch_b[l], w_out[l])
        h = _rmsnorm(x, norm_ffn_g[l]) * (1.0 + sc2[:, None, :]) + sh2[:, None, :]
        x = x + g2[:, None, :] * _moe(h, w_router[l], b_router[l], w_exp_gate[l], b_exp_gate[l],
                                      w_exp_up[l], b_exp_up[l], w_exp_down[l], b_exp_down[l])
    return _rmsnorm(x, norm_final_g)
```

```python
import functools

import jax
import jax.numpy as jnp
from jax import lax
from jax.experimental import pallas as pl
from jax.experimental.pallas import tpu as pltpu

GLA_HEADS = 4
GLA_CHUNK = 64
GLA_TAU = 16.0
SGU_CHUNK = 128
TOP_K = 4
SWIGLU_ALPHA = 1.702
SWIGLU_LIMIT = 7.0
EPS = 1e-5

LANES = 128
MOE_ROWS = 512
VMEM_LIMIT = 56 * 1024 * 1024

F32 = jnp.float32
BF16 = jnp.bfloat16
U32 = jnp.uint32
I32 = jnp.int32


def _params(*sem):
    return pltpu.CompilerParams(dimension_semantics=sem, vmem_limit_bytes=VMEM_LIMIT)


def _tile(n, want):
    if n <= want:
        return n
    t = want
    while t >= LANES:
        if n % t == 0:
            return t
        t -= LANES
    return n


def _dot(a, b):
    return jnp.dot(a, b, preferred_element_type=F32)


def _dot_nt(a, b):
    return lax.dot_general(a, b, (((1,), (1,)), ((), ())), preferred_element_type=F32)


def _dot_tn(a, b):
    return lax.dot_general(a, b, (((0,), (0,)), ((), ())), preferred_element_type=F32)


def _sigmoid(x):
    return 1.0 / (1.0 + jnp.exp(-x))


def _pack_halves(y):
    half = y.shape[1] // 2
    bits = lax.bitcast_convert_type(y.astype(BF16).astype(F32), U32)
    return (bits[:, :half] >> 16) | bits[:, half:]


def _unpack_lo(p):
    return lax.bitcast_convert_type(p << 16, F32)


def _unpack_hi(p):
    return lax.bitcast_convert_type(p & jnp.uint32(0xFFFF0000), F32)


def _ada_kernel(c_ref, w_ref, b_ref, o_ref):
    c = c_ref[...]
    s = c * _sigmoid(c)
    o_ref[...] = _dot(s.astype(BF16), w_ref[...].astype(BF16)) + b_ref[...]


def _ada(c_pad, w, b):
    rows, d = c_pad.shape
    n = w.shape[1]
    tn = _tile(n, 512)
    return pl.pallas_call(
        _ada_kernel,
        grid=(n // tn,),
        in_specs=[pl.BlockSpec((rows, d), lambda j: (0, 0)),
                  pl.BlockSpec((d, tn), lambda j: (0, j)),
                  pl.BlockSpec((1, tn), lambda j: (0, j))],
        out_specs=pl.BlockSpec((rows, tn), lambda j: (0, j)),
        out_shape=jax.ShapeDtypeStruct((rows, n), F32),
        compiler_params=_params("arbitrary"),
        name="ada_mod",
    )(c_pad, w, b)


def _norm_mod_kernel(x_ref, g_ref, sc_ref, sh_ref, o_ref):
    x = x_ref[...]
    inv = lax.rsqrt(jnp.mean(x * x, axis=-1, keepdims=True) + EPS)
    h = (x * inv) * g_ref[...] * (1.0 + sc_ref[...]) + sh_ref[...]
    o_ref[...] = h.astype(o_ref.dtype)


def _norm_mod(x2d, g, sc, sh, seq):
    t, d = x2d.shape
    tm = _tile(seq, 256)
    bpb = seq // tm
    return pl.pallas_call(
        _norm_mod_kernel,
        grid=(t // tm,),
        in_specs=[pl.BlockSpec((tm, d), lambda i: (i, 0)),
                  pl.BlockSpec((1, d), lambda i: (0, 0)),
                  pl.BlockSpec((None, 1, d), lambda i: (i // bpb, 0, 0)),
                  pl.BlockSpec((None, 1, d), lambda i: (i // bpb, 0, 0))],
        out_specs=pl.BlockSpec((tm, d), lambda i: (i, 0)),
        out_shape=jax.ShapeDtypeStruct((t, d), BF16),
        compiler_params=_params("arbitrary"),
        name="norm_mod",
    )(x2d, g, sc, sh)


def _mm_kernel(a_ref, w_ref, o_ref):
    o_ref[...] = _dot(a_ref[...], w_ref[...].astype(BF16)).astype(o_ref.dtype)


def _matmul(a, w, tm, tn):
    m, k = a.shape
    n = w.shape[1]
    return pl.pallas_call(
        _mm_kernel,
        grid=(m // tm, n // tn),
        in_specs=[pl.BlockSpec((tm, k), lambda i, j: (i, 0)),
                  pl.BlockSpec((k, tn), lambda i, j: (0, j))],
        out_specs=pl.BlockSpec((tm, tn), lambda i, j: (i, j)),
        out_shape=jax.ShapeDtypeStruct((m, n), BF16),
        compiler_params=_params("arbitrary", "arbitrary"),
        name="proj_qkvg",
    )(a, w)


def _mm_act_kernel(a_ref, w_ref, o_ref, *, n_gelu_tiles):
    acc = _dot(a_ref[...], w_ref[...].astype(BF16))
    j = pl.program_id(1)

    @pl.when(j < n_gelu_tiles)
    def _():
        o_ref[...] = jax.nn.gelu(acc).astype(o_ref.dtype)

    @pl.when(j >= n_gelu_tiles)
    def _():
        o_ref[...] = _sigmoid(acc).astype(o_ref.dtype)


def _matmul_act(a, w, tm, tn, n_gelu_cols):
    m, k = a.shape
    n = w.shape[1]
    assert n_gelu_cols % tn == 0
    return pl.pallas_call(
        functools.partial(_mm_act_kernel, n_gelu_tiles=n_gelu_cols // tn),
        grid=(m // tm, n // tn),
        in_specs=[pl.BlockSpec((tm, k), lambda i, j: (i, 0)),
                  pl.BlockSpec((k, tn), lambda i, j: (0, j))],
        out_specs=pl.BlockSpec((tm, tn), lambda i, j: (i, j)),
        out_shape=jax.ShapeDtypeStruct((m, n), BF16),
        compiler_params=_params("arbitrary", "arbitrary"),
        name="proj_uzgates",
    )(a, w)


def _alpha_kernel(h_ref, walr_ref, wup_ref, b_ref, o_ref):
    a = _dot(h_ref[...], walr_ref[...])
    z = _dot(a.astype(BF16), wup_ref[...]) + b_ref[...]
    log_sig = jnp.minimum(z, 0.0) - jnp.log1p(jnp.exp(-jnp.abs(z)))
    o_ref[...] = log_sig / GLA_TAU


def _alpha(h, walr_pad, wup_pad, b_alpha):
    t, d = h.shape
    key = wup_pad.shape[1]
    tm = _tile(t, 1024)
    return pl.pallas_call(
        _alpha_kernel,
        grid=(t // tm,),
        in_specs=[pl.BlockSpec((tm, d), lambda i: (i, 0)),
                  pl.BlockSpec((d, LANES), lambda i: (0, 0)),
                  pl.BlockSpec((LANES, key), lambda i: (0, 0)),
                  pl.BlockSpec((1, key), lambda i: (0, 0))],
        out_specs=pl.BlockSpec((tm, key), lambda i: (i, 0)),
        out_shape=jax.ShapeDtypeStruct((t, key), F32),
        compiler_params=_params("arbitrary"),
        name="gla_log_alpha",
    )(h, walr_pad, wup_pad, b_alpha)


def _gla_kernel(q_ref, k_ref, v_ref, la_ref, g_ref, gn_ref, o_ref, st_ref, *, n_chunks, scale):
    @pl.when(pl.program_id(2) == 0)
    def _():
        st_ref[...] = jnp.zeros_like(st_ref)

    c_len = GLA_CHUNK
    row = lax.broadcasted_iota(I32, (c_len, c_len), 0)
    col = lax.broadcasted_iota(I32, (c_len, c_len), 1)
    causal = col <= row
    tri = jnp.where(causal, 1.0, 0.0).astype(BF16)
    for c in range(n_chunks):
        sl = slice(c * c_len, (c + 1) * c_len)
        la = la_ref[sl, :]
        la_hi = la.astype(BF16)
        la_lo = (la - la_hi.astype(F32)).astype(BF16)
        cum = _dot(tri, la_hi) + _dot(tri, la_lo)
        cum_last = cum[c_len - 1:c_len, :]
        q = q_ref[sl, :].astype(F32) * scale
        k = k_ref[sl, :].astype(F32)
        v = v_ref[sl, :]
        q_abs = (q * jnp.exp(cum)).astype(BF16)
        q_rel = (q * jnp.exp(cum - cum_last)).astype(BF16)
        k_rel = (k * jnp.exp(cum_last - cum)).astype(BF16)
        decay = jnp.exp(cum_last)
        scores = jnp.where(causal, _dot_nt(q_rel, k_rel), 0.0).astype(BF16)
        st = st_ref[...]
        o = _dot(scores, v) + _dot_nt(q_abs, st.astype(BF16))
        st_ref[...] = st * decay + _dot_tn(v, k_rel)
        inv = lax.rsqrt(jnp.mean(o * o, axis=-1, keepdims=True) + EPS)
        g = g_ref[sl, :].astype(F32)
        o_ref[sl, :] = (o * inv * gn_ref[...] * (g * _sigmoid(g))).astype(o_ref.dtype)


def _gla(qkvg, log_a, gn, batch, seq, key, val):
    t = qkvg.shape[0]
    dk, dv = key // GLA_HEADS, val // GLA_HEADS
    blk = _tile(seq, 256)
    assert blk % GLA_CHUNK == 0 and key % dv == 0
    nb = seq // blk
    kq, kv, kg = key // dk, 2 * key // dv, (2 * key + val) // dv
    row = lambda b, h, n: b * nb + n
    return pl.pallas_call(
        functools.partial(_gla_kernel, n_chunks=blk // GLA_CHUNK, scale=float(dk) ** -0.5),
        grid=(batch, GLA_HEADS, nb),
        in_specs=[pl.BlockSpec((blk, dk), lambda b, h, n: (row(b, h, n), h)),
                  pl.BlockSpec((blk, dk), lambda b, h, n: (row(b, h, n), kq + h)),
                  pl.BlockSpec((blk, dv), lambda b, h, n: (row(b, h, n), kv + h)),
                  pl.BlockSpec((blk, dk), lambda b, h, n: (row(b, h, n), h)),
                  pl.BlockSpec((blk, dv), lambda b, h, n: (row(b, h, n), kg + h)),
                  pl.BlockSpec((1, dv), lambda b, h, n: (0, 0))],
        out_specs=pl.BlockSpec((blk, dv), lambda b, h, n: (row(b, h, n), h)),
        out_shape=jax.ShapeDtypeStruct((t, val), BF16),
        scratch_shapes=[pltpu.VMEM((dv, dk), F32)],
        compiler_params=_params("arbitrary", "arbitrary", "arbitrary"),
        name="gla_mixer",
    )(qkvg, qkvg, qkvg, log_a, qkvg, gn)


def _sgu_kernel(u_ref, z_ref, lg_ref, lb_ref, w_ref, bs_ref, o_ref, *, n_chunks, groups):
    c_len = SGU_CHUNK
    gd = u_ref.shape[1] // groups
    row = lax.broadcasted_iota(I32, (c_len, c_len), 0)
    col = lax.broadcasted_iota(I32, (c_len, c_len), 1)
    causal = col <= row
    w_masked = [jnp.where(causal, w_ref[g], 0.0).astype(BF16) for g in range(groups)]
    for c in range(n_chunks):
        sl = slice(c * c_len, (c + 1) * c_len)
        z = z_ref[sl, :].astype(F32)
        mu = jnp.mean(z, axis=-1, keepdims=True)
        zc = z - mu
        var = jnp.mean(zc * zc, axis=-1, keepdims=True)
        zn = (zc * lax.rsqrt(var + EPS) * lg_ref[...] + lb_ref[...]).astype(BF16)
        for g in range(groups):
            gs = slice(g * gd, (g + 1) * gd)
            s = _dot(w_masked[g], zn[:, gs]) + bs_ref[:, gs]
            o_ref[sl, gs] = (u_ref[sl, gs].astype(F32) * s).astype(o_ref.dtype)


def _sgu(uzg, ln_g, ln_b, w_spatial, bs_full, seq, width):
    t = uzg.shape[0]
    groups = w_spatial.shape[0]
    blk = _tile(seq, 256)
    assert blk % SGU_CHUNK == 0
    return pl.pallas_call(
        functools.partial(_sgu_kernel, n_chunks=blk // SGU_CHUNK, groups=groups),
        grid=(t // blk,),
        in_specs=[pl.BlockSpec((blk, width), lambda i: (i, 0)),
                  pl.BlockSpec((blk, width), lambda i: (i, 1)),
                  pl.BlockSpec((1, width), lambda i: (0, 0)),
                  pl.BlockSpec((1, width), lambda i: (0, 0)),
                  pl.BlockSpec((groups, SGU_CHUNK, SGU_CHUNK), lambda i: (0, 0, 0)),
                  pl.BlockSpec((SGU_CHUNK, width), lambda i: (0, 0))],
        out_specs=pl.BlockSpec((blk, width), lambda i: (i, 0)),
        out_shape=jax.ShapeDtypeStruct((t, width), BF16),
        compiler_params=_params("arbitrary"),
        name="sgu_mixer",
    )(uzg, uzg, ln_g, ln_b, w_spatial, bs_full)


def _merge_kernel(a_ref, b_ref, wa_ref, wb_ref, ga_ref, gb_ref, o_ref):
    ya = _dot(a_ref[...], wa_ref[...].astype(BF16))
    yb = _dot(b_ref[...], wb_ref[...].astype(BF16))
    o_ref[...] = (ga_ref[...].astype(F32) * ya + gb_ref[...].astype(F32) * yb).astype(o_ref.dtype)


def _merge(a, b, wa, wb, uzg, width, tm, tn):
    t, val = a.shape
    d = wa.shape[1]
    ga0, gb0 = 2 * width // tn, (2 * width + d) // tn
    return pl.pallas_call(
        _merge_kernel,
        grid=(t // tm, d // tn),
        in_specs=[pl.BlockSpec((tm, val), lambda i, j: (i, 0)),
                  pl.BlockSpec((tm, width), lambda i, j: (i, 0)),
                  pl.BlockSpec((val, tn), lambda i, j: (0, j)),
                  pl.BlockSpec((width, tn), lambda i, j: (0, j)),
                  pl.BlockSpec((tm, tn), lambda i, j: (i, ga0 + j)),
                  pl.BlockSpec((tm, tn), lambda i, j: (i, gb0 + j))],
        out_specs=pl.BlockSpec((tm, tn), lambda i, j: (i, j)),
        out_shape=jax.ShapeDtypeStruct((t, d), BF16),
        compiler_params=_params("arbitrary", "arbitrary"),
        name="merge_branches",
    )(a, b, wa, wb, uzg, uzg)


def _outproj_kernel(m_ref, w_ref, x_ref, g1_ref, o_ref):
    y = _dot(m_ref[...], w_ref[...].astype(BF16))
    o_ref[...] = x_ref[...] + g1_ref[...] * y


def _outproj(merged, w, x2d, g1, seq, tm, tn):
    t, d = x2d.shape
    bpb = seq // tm
    return pl.pallas_call(
        _outproj_kernel,
        grid=(t // tm, d // tn),
        in_specs=[pl.BlockSpec((tm, d), lambda i, j: (i, 0)),
                  pl.BlockSpec((d, tn), lambda i, j: (0, j)),
                  pl.BlockSpec((tm, tn), lambda i, j: (i, j)),
                  pl.BlockSpec((None, 1, tn), lambda i, j: (i // bpb, 0, j))],
        out_specs=pl.BlockSpec((tm, tn), lambda i, j: (i, j)),
        out_shape=jax.ShapeDtypeStruct((t, d), F32),
        compiler_params=_params("arbitrary", "arbitrary"),
        name="out_proj_residual",
    )(merged, w, x2d, g1)


def _router_kernel(x_ref, g_ref, sc_ref, sh_ref, wr_ref, br_ref,
                   hp_ref, idx_ref, wt_ref, rank_ref, cnt_ref, carry_ref, *, n_exp):
    @pl.when(pl.program_id(0) == 0)
    def _():
        carry_ref[...] = jnp.zeros_like(carry_ref)

    x = x_ref[...]
    tm = x.shape[0]
    inv = lax.rsqrt(jnp.mean(x * x, axis=-1, keepdims=True) + EPS)
    h = (x * inv) * g_ref[...] * (1.0 + sc_ref[...]) + sh_ref[...]
    hp_ref[...] = _pack_halves(h)
    h_hi = h.astype(BF16)
    h_lo = (h - h_hi.astype(F32)).astype(BF16)
    p = _dot(h_hi, wr_ref[...])
    q = _dot(h_lo, wr_ref[...])
    lane = lax.broadcasted_iota(I32, (tm, LANES), 1)
    logits = p + pltpu.roll(p, LANES - n_exp, 1) + q + br_ref[...]
    neg_inf = jnp.float32(-jnp.inf)
    work = jnp.where(lane < n_exp, logits, neg_inf)
    vals, sels = [], []
    idx_out = jnp.zeros((tm, LANES), I32)
    for k in range(TOP_K):
        m = jnp.max(work, axis=-1, keepdims=True)
        idx = jnp.min(jnp.where(work == m, lane, LANES), axis=-1, keepdims=True)
        sel = lane == idx
        work = jnp.where(sel, neg_inf, work)
        vals.append(m)
        sels.append(sel)
        idx_out = jnp.where(lane == k, idx, idx_out)
    exps = [jnp.exp(v - vals[0]) for v in vals]
    denom = exps[0]
    for e in exps[1:]:
        denom = denom + e
    wt_out = jnp.zeros((tm, LANES), F32)
    for k in range(TOP_K):
        wt_out = jnp.where(lane == k, exps[k] / denom, wt_out)
    onehot = jnp.zeros((tm, LANES), F32)
    for sel in sels:
        onehot = jnp.where(sel, 1.0, onehot)
    r = lax.broadcasted_iota(I32, (tm, tm), 0)
    c = lax.broadcasted_iota(I32, (tm, tm), 1)
    strict = jnp.where(c < r, 1.0, 0.0).astype(BF16)
    before = carry_ref[...] + _dot(strict, onehot.astype(BF16))
    rank_out = jnp.zeros((tm, LANES), I32)
    for k in range(TOP_K):
        rk = jnp.sum(jnp.where(sels[k], before, 0.0), axis=-1, keepdims=True)
        rank_out = jnp.where(lane == k, rk.astype(I32), rank_out)
    carry_ref[...] = carry_ref[...] + jnp.sum(onehot, axis=0, keepdims=True)
    idx_ref[...] = idx_out
    wt_ref[...] = wt_out
    rank_ref[...] = rank_out
    cnt_ref[...] = carry_ref[...]


def _router(x1, g, sc, sh, wr_cat, br_pad, seq, n_exp):
    t, d = x1.shape
    tm = _tile(seq, 256)
    bpb = seq // tm
    tok = pl.BlockSpec((tm, LANES), lambda i: (i, 0))
    return pl.pallas_call(
        functools.partial(_router_kernel, n_exp=n_exp),
        grid=(t // tm,),
        in_specs=[pl.BlockSpec((tm, d), lambda i: (i, 0)),
                  pl.BlockSpec((1, d), lambda i: (0, 0)),
                  pl.BlockSpec((None, 1, d), lambda i: (i // bpb, 0, 0)),
                  pl.BlockSpec((None, 1, d), lambda i: (i // bpb, 0, 0)),
                  pl.BlockSpec((d, LANES), lambda i: (0, 0)),
                  pl.BlockSpec((1, LANES), lambda i: (0, 0))],
        out_specs=[pl.BlockSpec((tm, d // 2), lambda i: (i, 0)), tok, tok, tok,
                   pl.BlockSpec((1, LANES), lambda i: (0, 0))],
        out_shape=[jax.ShapeDtypeStruct((t, d // 2), U32),
                   jax.ShapeDtypeStruct((t, LANES), I32),
                   jax.ShapeDtypeStruct((t, LANES), F32),
                   jax.ShapeDtypeStruct((t, LANES), I32),
                   jax.ShapeDtypeStruct((1, LANES), F32)],
        scratch_shapes=[pltpu.VMEM((1, LANES), F32)],
        compiler_params=_params("arbitrary"),
        name="norm_router_topk",
    )(x1, g, sc, sh, wr_cat, br_pad)


def _dispatch_kernel(pos_ref, hp_ref, xs_in_ref, xs_ref, sem, *, tm):
    del xs_in_ref

    def row_copy(r, p):
        return pltpu.make_async_copy(hp_ref.at[pl.ds(r, 1)], xs_ref.at[pl.ds(p, 1)], sem)

    def start(r, carry):
        for k in range(TOP_K):
            row_copy(r, pos_ref[r * TOP_K + k]).start()
        return carry

    def wait(r, carry):
        for k in range(TOP_K):
            row_copy(r, pos_ref[r * TOP_K + k]).wait()
        return carry

    lax.fori_loop(0, tm, start, 0)
    lax.fori_loop(0, tm, wait, 0)


def _dispatch(pos_flat, hp, xs_zero):
    t, half = hp.shape
    tm = _tile(t, 256)
    return pl.pallas_call(
        functools.partial(_dispatch_kernel, tm=tm),
        grid=(t // tm,),
        in_specs=[pl.BlockSpec((tm * TOP_K,), lambda i: (i,), memory_space=pltpu.SMEM),
                  pl.BlockSpec((tm, half), lambda i: (i, 0)),
                  pl.BlockSpec(memory_space=pl.ANY)],
        out_specs=pl.BlockSpec(memory_space=pl.ANY),
        out_shape=jax.ShapeDtypeStruct(xs_zero.shape, U32),
        scratch_shapes=[pltpu.SemaphoreType.DMA(())],
        input_output_aliases={2: 0},
        compiler_params=_params("arbitrary"),
        name="moe_dispatch",
    )(pos_flat, hp, xs_zero)


def _expert_up_kernel(be_ref, nv_ref, x_ref, wg_ref, bg_ref, wu_ref, bu_ref, h_ref):
    del be_ref
    valid = pl.program_id(1) < nv_ref[0]

    @pl.when(valid)
    def _():
        xp = x_ref[...]
        half = xp.shape[1]
        x_lo = _unpack_lo(xp).astype(BF16)
        x_hi = _unpack_hi(xp).astype(BF16)
        wg = wg_ref[...].astype(BF16)
        wu = wu_ref[...].astype(BF16)
        glu = _dot(x_lo, wg[:half]) + _dot(x_hi, wg[half:]) + bg_ref[...]
        lin = _dot(x_lo, wu[:half]) + _dot(x_hi, wu[half:]) + bu_ref[...]
        glu = jnp.minimum(glu, SWIGLU_LIMIT)
        lin = jnp.clip(lin, -SWIGLU_LIMIT, SWIGLU_LIMIT)
        h_ref[...] = (glu * _sigmoid(SWIGLU_ALPHA * glu) * (lin + 1.0)).astype(h_ref.dtype)

    @pl.when(jnp.logical_not(valid))
    def _():
        h_ref[...] = jnp.zeros_like(h_ref)


def _expert_up(blk_exp, n_valid, xs, wg, bg, wu, bu, tf):
    n_rows, half = xs.shape
    d, f = wg.shape[-2], wg.shape[-1]
    n_blocks = n_rows // MOE_ROWS
    last = lambda i, nv: jnp.minimum(i, nv[0] - 1)
    grid_spec = pltpu.PrefetchScalarGridSpec(
        num_scalar_prefetch=2,
        grid=(f // tf, n_blocks),
        in_specs=[pl.BlockSpec((MOE_ROWS, half), lambda j, i, be, nv: (last(i, nv), 0)),
                  pl.BlockSpec((None, None, d, tf), lambda j, i, be, nv: (0, be[i], 0, j)),
                  pl.BlockSpec((None, 1, tf), lambda j, i, be, nv: (be[i], 0, j)),
                  pl.BlockSpec((None, None, d, tf), lambda j, i, be, nv: (0, be[i], 0, j)),
                  pl.BlockSpec((None, 1, tf), lambda j, i, be, nv: (be[i], 0, j))],
        out_specs=pl.BlockSpec((MOE_ROWS, tf), lambda j, i, be, nv: (i, j)),
    )
    return pl.pallas_call(
        _expert_up_kernel,
        grid_spec=grid_spec,
        out_shape=jax.ShapeDtypeStruct((n_rows, f), BF16),
        compiler_params=_params("arbitrary", "arbitrary"),
        name="expert_up",
    )(blk_exp, n_valid, xs, wg, bg, wu, bu)


def _expert_down_kernel(be_ref, nv_ref, h_ref, wd_ref, bd_ref, o_ref):
    del be_ref
    valid = pl.program_id(1) < nv_ref[0]

    @pl.when(valid)
    def _():
        y = _dot(h_ref[...], wd_ref[...].astype(BF16)) + bd_ref[...]
        o_ref[...] = _pack_halves(y)

    @pl.when(jnp.logical_not(valid))
    def _():
        o_ref[...] = jnp.zeros_like(o_ref)


def _expert_down(blk_exp, n_valid, hid, wd, bd, tn):
    n_rows, f = hid.shape
    d = wd.shape[-1]
    n_blocks = n_rows // MOE_ROWS
    last = lambda i, nv: jnp.minimum(i, nv[0] - 1)
    grid_spec = pltpu.PrefetchScalarGridSpec(
        num_scalar_prefetch=2,
        grid=(d // tn, n_blocks),
        in_specs=[pl.BlockSpec((MOE_ROWS, f), lambda j, i, be, nv: (last(i, nv), 0)),
                  pl.BlockSpec((None, None, f, tn), lambda j, i, be, nv: (0, be[i], 0, j)),
                  pl.BlockSpec((None, 1, tn), lambda j, i, be, nv: (be[i], 0, j))],
        out_specs=pl.BlockSpec((MOE_ROWS, tn // 2), lambda j, i, be, nv: (i, j)),
    )
    return pl.pallas_call(
        _expert_down_kernel,
        grid_spec=grid_spec,
        out_shape=jax.ShapeDtypeStruct((n_rows, d // 2), U32),
        compiler_params=_params("arbitrary", "arbitrary"),
        name="expert_down",
    )(blk_exp, n_valid, hid, wd, bd)


def _combine_kernel(pos_ref, wt_ref, x_ref, g2_ref, gf_ref, ys_ref, o_ref, buf_ref, sem, *, tm, tn):
    def row_copy(r, k, p):
        return pltpu.make_async_copy(ys_ref.at[pl.ds(p, 1)], buf_ref.at[k, pl.ds(r, 1)], sem)

    def start(r, carry):
        for k in range(TOP_K):
            row_copy(r, k, pos_ref[r * TOP_K + k]).start()
        return carry

    def wait(r, carry):
        for k in range(TOP_K):
            row_copy(r, k, pos_ref[r * TOP_K + k]).wait()
        return carry

    lax.fori_loop(0, tm, start, 0)
    lax.fori_loop(0, tm, wait, 0)

    d = x_ref.shape[1]
    hw = tn // 2
    wts = [wt_ref[:, k:k + 1] for k in range(TOP_K)]
    pieces = []
    ssq = jnp.zeros((tm, 1), F32)
    for j in range(d // tn):
        for part, unpack in ((0, _unpack_lo), (1, _unpack_hi)):
            cols = slice(j * tn + part * hw, j * tn + (part + 1) * hw)
            y = jnp.zeros((tm, hw), F32)
            for k in range(TOP_K):
                y = y + wts[k] * unpack(buf_ref[k, :, j * hw:(j + 1) * hw])
            x2 = x_ref[:, cols] + g2_ref[:, cols] * y
            ssq = ssq + jnp.sum(x2 * x2, axis=-1, keepdims=True)
            pieces.append((cols, x2))
    inv = lax.rsqrt(ssq / d + EPS)
    for cols, x2 in pieces:
        o_ref[:, cols] = x2 * inv * gf_ref[:, cols]


def _combine(pos_flat, wt, x1, g2, gf, ys, seq, tn):
    t, d = x1.shape
    tm = _tile(seq, 128)
    bpb = seq // tm
    return pl.pallas_call(
        functools.partial(_combine_kernel, tm=tm, tn=tn),
        grid=(t // tm,),
        in_specs=[pl.BlockSpec((tm * TOP_K,), lambda i: (i,), memory_space=pltpu.SMEM),
                  pl.BlockSpec((tm, LANES), lambda i: (i, 0)),
                  pl.BlockSpec((tm, d), lambda i: (i, 0)),
                  pl.BlockSpec((None, 1, d), lambda i: (i // bpb, 0, 0)),
                  pl.BlockSpec((1, d), lambda i: (0, 0)),
                  pl.BlockSpec(memory_space=pl.ANY)],
        out_specs=pl.BlockSpec((tm, d), lambda i: (i, 0)),
        out_shape=jax.ShapeDtypeStruct((t, d), F32),
        scratch_shapes=[pltpu.VMEM((TOP_K, tm, d // 2), U32), pltpu.SemaphoreType.DMA(())],
        compiler_params=_params("arbitrary"),
        name="moe_combine_final_norm",
    )(pos_flat, wt, x1, g2, gf, ys)


def _layer(x2d, c_pad, batch, seq, w_ada, b_ada, norm_mix_g, w_in, w_alpha_up, b_alpha, gla_norm_g, sgu_ln_g,
           sgu_ln_b, w_spatial, b_spatial, w_branch_a, w_branch_b, w_out, norm_ffn_g, w_router, b_router,
           w_exp_gate, b_exp_gate, w_exp_up, b_exp_up, w_exp_down, b_exp_down, final_g):
    t, d = x2d.shape
    rank, key = w_alpha_up.shape
    val = w_branch_a.shape[0]
    width = w_branch_b.shape[0]
    n_exp = w_router.shape[1]
    f = w_exp_gate.shape[-1]
    assert 2 * n_exp <= LANES and rank <= LANES

    mod = _ada(c_pad, w_ada, b_ada.reshape(1, -1))[:batch].reshape(batch, 6, 1, d)
    sh1, sc1, g1, sh2, sc2, g2 = (mod[:, i] for i in range(6))

    h = _norm_mod(x2d, norm_mix_g.reshape(1, d), sc1, sh1, seq)

    tm = _tile(t, 1024)
    n_qkvg = 2 * key + 2 * val
    qkvg = _matmul(h, w_in[:, :n_qkvg].astype(BF16), tm, _tile(n_qkvg, 512))
    walr = jnp.pad(w_in[:, n_qkvg:n_qkvg + rank], ((0, 0), (0, LANES - rank))).astype(BF16)
    wup = jnp.pad(w_alpha_up, ((0, LANES - rank), (0, 0))).astype(BF16)
    log_a = _alpha(h, walr, wup, b_alpha.reshape(1, key))
    uzg = _matmul_act(h, w_in[:, n_qkvg + rank:].astype(BF16), tm, _tile(width, 512), 2 * width)

    a = _gla(qkvg, log_a, gla_norm_g.reshape(1, -1), batch, seq, key, val)
    bs_full = jnp.repeat(b_spatial.T, width // w_spatial.shape[0], axis=1)
    b = _sgu(uzg, sgu_ln_g.reshape(1, width), sgu_ln_b.reshape(1, width), w_spatial, bs_full, seq, width)

    tn = _tile(d, 512)
    merged = _merge(a, b, w_branch_a.astype(BF16), w_branch_b.astype(BF16), uzg, width, tm, tn)
    x1 = _outproj(merged, w_out.astype(BF16), x2d, g1, seq, _tile(seq, 1024), tn)

    w_hi = w_router.astype(BF16)
    w_lo = (w_router - w_hi.astype(F32)).astype(BF16)
    wr_cat = jnp.pad(jnp.concatenate([w_hi, w_lo], axis=1), ((0, 0), (0, LANES - 2 * n_exp)))
    br_pad = jnp.pad(b_router.reshape(1, n_exp), ((0, 0), (0, LANES - n_exp)))
    hp, top_idx, top_w, rank_in_exp, counts = _router(x1, norm_ffn_g.reshape(1, d), sc2, sh2, wr_cat, br_pad,
                                                      seq, n_exp)

    counts = counts[0, :n_exp].astype(I32)
    padded = (counts + MOE_ROWS - 1) // MOE_ROWS * MOE_ROWS
    pad_end = jnp.cumsum(padded)
    pad_start = pad_end - padded
    top_idx = top_idx[:, :TOP_K]
    pos_flat = (pad_start[top_idx] + rank_in_exp[:, :TOP_K]).reshape(-1)
    n_rows = t * TOP_K + n_exp * MOE_ROWS
    n_blocks = n_rows // MOE_ROWS
    blk_start = jnp.arange(n_blocks, dtype=I32) * MOE_ROWS
    blk_exp = jnp.minimum(jnp.sum(blk_start[:, None] >= pad_end[None, :], axis=1), n_exp - 1).astype(I32)
    n_valid = (pad_end[-1:] // MOE_ROWS).astype(I32)

    xs = _dispatch(pos_flat, hp, jnp.zeros((n_rows, d // 2), U32))
    hid = _expert_up(blk_exp, n_valid, xs, w_exp_gate, b_exp_gate.reshape(n_exp, 1, f),
                     w_exp_up, b_exp_up.reshape(n_exp, 1, f), _tile(f, 256))
    tn_down = _tile(d, 2048)
    ys = _expert_down(blk_exp, n_valid, hid, w_exp_down, b_exp_down.reshape(n_exp, 1, d), tn_down)
    return _combine(pos_flat, top_w, x1, g2, final_g, ys, seq, tn_down)


def kernel(x, c, w_ada, b_ada, norm_mix_g, w_in, w_alpha_up, b_alpha, gla_norm_g, sgu_ln_g, sgu_ln_b,
           w_spatial, b_spatial, w_branch_a, w_branch_b, w_out, norm_ffn_g, w_router, b_router,
           w_exp_gate, b_exp_gate, w_exp_up, b_exp_up, w_exp_down, b_exp_down, norm_final_g):
    batch, seq, d = x.shape
    depth = w_ada.shape[0]
    assert depth == 1, "the final rmsnorm is fused into the last layer's combine kernel"
    c_pad = jnp.pad(c, ((0, 8 - batch % 8), (0, 0))) if batch % 8 else c
    out = _layer(x.reshape(batch * seq, d), c_pad, batch, seq, w_ada[0], b_ada[0], norm_mix_g[0], w_in[0],
                 w_alpha_up[0], b_alpha[0], gla_norm_g[0], sgu_ln_g[0], sgu_ln_b[0], w_spatial[0],
                 b_spatial[0], w_branch_a[0], w_branch_b[0], w_out[0], norm_ffn_g[0], w_router[0],
                 b_router[0], w_exp_gate, b_exp_gate[0], w_exp_up, b_exp_up[0], w_exp_down, b_exp_down[0],
                 norm_final_g.reshape(1, d))
    return out.reshape(batch, seq, d)
```

```python
import functools

import jax
import jax.numpy as jnp
from jax import lax
from jax.experimental import pallas as pl
from jax.experimental.pallas import tpu as pltpu

GLA_HEADS = 4
GLA_CHUNK = 64
GLA_TAU = 16.0
SGU_CHUNK = 128
TOP_K = 4
SWIGLU_ALPHA = 1.702
SWIGLU_LIMIT = 7.0
EPS = 1e-5

LANES = 128
MOE_ROWS = 512
VMEM_LIMIT = 56 * 1024 * 1024

F32 = jnp.float32
BF16 = jnp.bfloat16
U32 = jnp.uint32
I32 = jnp.int32


def _params(*sem):
    return pltpu.CompilerParams(dimension_semantics=sem, vmem_limit_bytes=VMEM_LIMIT)


def _tile(n, want):
    if n <= want:
        return n
    t = want
    while t >= LANES:
        if n % t == 0:
            return t
        t -= LANES
    return n


def _dot(a, b):
    return jnp.dot(a, b, preferred_element_type=F32)


def _dot_nt(a, b):
    return lax.dot_general(a, b, (((1,), (1,)), ((), ())), preferred_element_type=F32)


def _dot_tn(a, b):
    return lax.dot_general(a, b, (((0,), (0,)), ((), ())), preferred_element_type=F32)


def _sigmoid(x):
    return 1.0 / (1.0 + jnp.exp(-x))


def _pack_halves(y):
    half = y.shape[1] // 2
    bits = lax.bitcast_convert_type(y.astype(BF16).astype(F32), U32)
    return (bits[:, :half] >> 16) | bits[:, half:]


def _unpack_lo(p):
    return lax.bitcast_convert_type(p << 16, F32)


def _unpack_hi(p):
    return lax.bitcast_convert_type(p & jnp.uint32(0xFFFF0000), F32)


def _ada_kernel(c_ref, w_ref, b_ref, o_ref):
    c = c_ref[...]
    s = c * _sigmoid(c)
    o_ref[...] = _dot(s.astype(BF16), w_ref[...].astype(BF16)) + b_ref[...]


def _ada(c_pad, w, b):
    rows, d = c_pad.shape
    n = w.shape[1]
    tn = _tile(n, 512)
    return pl.pallas_call(
        _ada_kernel,
        grid=(n // tn,),
        in_specs=[pl.BlockSpec((rows, d), lambda j: (0, 0)),
                  pl.BlockSpec((d, tn), lambda j: (0, j)),
                  pl.BlockSpec((1, tn), lambda j: (0, j))],
        out_specs=pl.BlockSpec((rows, tn), lambda j: (0, j)),
        out_shape=jax.ShapeDtypeStruct((rows, n), F32),
        compiler_params=_params("arbitrary"),
        name="ada_mod",
    )(c_pad, w, b)


def _norm_mod_kernel(x_ref, g_ref, sc_ref, sh_ref, o_ref):
    x = x_ref[...]
    inv = lax.rsqrt(jnp.mean(x * x, axis=-1, keepdims=True) + EPS)
    h = (x * inv) * g_ref[...] * (1.0 + sc_ref[...]) + sh_ref[...]
    o_ref[...] = h.astype(o_ref.dtype)


def _norm_mod(x2d, g, sc, sh, seq):
    t, d = x2d.shape
    tm = _tile(seq, 256)
    bpb = seq // tm
    return pl.pallas_call(
        _norm_mod_kernel,
        grid=(t // tm,),
        in_specs=[pl.BlockSpec((tm, d), lambda i: (i, 0)),
                  pl.BlockSpec((1, d), lambda i: (0, 0)),
                  pl.BlockSpec((None, 1, d), lambda i: (i // bpb, 0, 0)),
                  pl.BlockSpec((None, 1, d), lambda i: (i // bpb, 0, 0))],
        out_specs=pl.BlockSpec((tm, d), lambda i: (i, 0)),
        out_shape=jax.ShapeDtypeStruct((t, d), BF16),
        compiler_params=_params("arbitrary"),
        name="norm_mod",
    )(x2d, g, sc, sh)


def _mm_act_kernel(a_ref, w_ref, o_ref, *, act):
    acc = _dot(a_ref[...], w_ref[...].astype(BF16))
    if act == "gelu":
        acc = jax.nn.gelu(acc)
    elif act == "sigmoid":
        acc = _sigmoid(acc)
    o_ref[...] = acc.astype(o_ref.dtype)


def _matmul_act(a, w, tm, tn, col0, n_cols, act, name):
    m, k = a.shape
    assert col0 % tn == 0 and n_cols % tn == 0
    j0 = col0 // tn
    return pl.pallas_call(
        functools.partial(_mm_act_kernel, act=act),
        grid=(m // tm, n_cols // tn),
        in_specs=[pl.BlockSpec((tm, k), lambda i, j: (i, 0)),
                  pl.BlockSpec((k, tn), lambda i, j: (0, j0 + j))],
        out_specs=pl.BlockSpec((tm, tn), lambda i, j: (i, j)),
        out_shape=jax.ShapeDtypeStruct((m, n_cols), BF16),
        compiler_params=_params("arbitrary", "arbitrary"),
        name=name,
    )(a, w)


def _alpha_kernel(h_ref, walr_ref, wup_ref, b_ref, o_ref):
    a = _dot(h_ref[...], walr_ref[...])
    z = _dot(a.astype(BF16), wup_ref[...]) + b_ref[...]
    log_sig = jnp.minimum(z, 0.0) - jnp.log1p(jnp.exp(-jnp.abs(z)))
    o_ref[...] = log_sig / GLA_TAU


def _alpha(h, walr_pad, wup_pad, b_alpha):
    t, d = h.shape
    key = wup_pad.shape[1]
    tm = _tile(t, 1024)
    return pl.pallas_call(
        _alpha_kernel,
        grid=(t // tm,),
        in_specs=[pl.BlockSpec((tm, d), lambda i: (i, 0)),
                  pl.BlockSpec((d, LANES), lambda i: (0, 0)),
                  pl.BlockSpec((LANES, key), lambda i: (0, 0)),
                  pl.BlockSpec((1, key), lambda i: (0, 0))],
        out_specs=pl.BlockSpec((tm, key), lambda i: (i, 0)),
        out_shape=jax.ShapeDtypeStruct((t, key), F32),
        compiler_params=_params("arbitrary"),
        name="gla_log_alpha",
    )(h, walr_pad, wup_pad, b_alpha)


def _gla_kernel(q_ref, k_ref, v_ref, la_ref, g_ref, gn_ref, o_ref, st_ref, *, n_chunks, scale):
    @pl.when(pl.program_id(1) == 0)
    def _():
        st_ref[...] = jnp.zeros_like(st_ref)

    c_len = GLA_CHUNK
    dk = q_ref.shape[1] // GLA_HEADS
    dv = v_ref.shape[1] // GLA_HEADS
    row = lax.broadcasted_iota(I32, (c_len, c_len), 0)
    col = lax.broadcasted_iota(I32, (c_len, c_len), 1)
    causal = col <= row
    tri = jnp.where(causal, 1.0, 0.0).astype(BF16)
    for c in range(n_chunks):
        sl = slice(c * c_len, (c + 1) * c_len)
        for hd in range(GLA_HEADS):
            ks = slice(hd * dk, (hd + 1) * dk)
            vs = slice(hd * dv, (hd + 1) * dv)
            la = la_ref[sl, ks]
            la_hi = la.astype(BF16)
            la_lo = (la - la_hi.astype(F32)).astype(BF16)
            cum = _dot(tri, la_hi) + _dot(tri, la_lo)
            cum_last = cum[c_len - 1:c_len, :]
            q = q_ref[sl, ks].astype(F32) * scale
            k = k_ref[sl, ks].astype(F32)
            v = v_ref[sl, vs]
            q_abs = (q * jnp.exp(cum)).astype(BF16)
            q_rel = (q * jnp.exp(cum - cum_last)).astype(BF16)
            k_rel = (k * jnp.exp(cum_last - cum)).astype(BF16)
            decay = jnp.exp(cum_last)
            scores = jnp.where(causal, _dot_nt(q_rel, k_rel), 0.0).astype(BF16)
            st = st_ref[hd]
            o = _dot(scores, v) + _dot_nt(q_abs, st.astype(BF16))
            st_ref[hd] = st * decay + _dot_tn(v, k_rel)
            inv = lax.rsqrt(jnp.mean(o * o, axis=-1, keepdims=True) + EPS)
            g = g_ref[sl, vs].astype(F32)
            o_ref[sl, vs] = (o * inv * gn_ref[...] * (g * _sigmoid(g))).astype(o_ref.dtype)


def _gla(qkvg, log_a, gn, batch, seq, key, val):
    t = qkvg.shape[0]
    dk, dv = key // GLA_HEADS, val // GLA_HEADS
    blk = _tile(seq, 128)
    assert blk % GLA_CHUNK == 0 and (2 * key) % val == 0
    nb = seq // blk
    row = lambda b, n: b * nb + n
    return pl.pallas_call(
        functools.partial(_gla_kernel, n_chunks=blk // GLA_CHUNK, scale=float(dk) ** -0.5),
        grid=(batch, nb),
        in_specs=[pl.BlockSpec((blk, key), lambda b, n: (row(b, n), 0)),
                  pl.BlockSpec((blk, key), lambda b, n: (row(b, n), 1)),
                  pl.BlockSpec((blk, val), lambda b, n: (row(b, n), 2 * key // val)),
                  pl.BlockSpec((blk, key), lambda b, n: (row(b, n), 0)),
                  pl.BlockSpec((blk, val), lambda b, n: (row(b, n), 2 * key // val + 1)),
                  pl.BlockSpec((1, dv), lambda b, n: (0, 0))],
        out_specs=pl.BlockSpec((blk, val), lambda b, n: (row(b, n), 0)),
        out_shape=jax.ShapeDtypeStruct((t, val), BF16),
        scratch_shapes=[pltpu.VMEM((GLA_HEADS, dv, dk), F32)],
        compiler_params=_params("arbitrary", "arbitrary"),
        name="gla_mixer",
    )(qkvg, qkvg, qkvg, log_a, qkvg, gn)


def _sgu_kernel(u_ref, z_ref, lg_ref, lb_ref, w_ref, bs_ref, o_ref, *, n_chunks, groups):
    c_len = SGU_CHUNK
    gd = u_ref.shape[1] // groups
    row = lax.broadcasted_iota(I32, (c_len, c_len), 0)
    col = lax.broadcasted_iota(I32, (c_len, c_len), 1)
    causal = col <= row
    w_masked = [jnp.where(causal, w_ref[g], 0.0).astype(BF16) for g in range(groups)]
    for c in range(n_chunks):
        sl = slice(c * c_len, (c + 1) * c_len)
        z = z_ref[sl, :].astype(F32)
        mu = jnp.mean(z, axis=-1, keepdims=True)
        zc = z - mu
        var = jnp.mean(zc * zc, axis=-1, keepdims=True)
        zn = (zc * lax.rsqrt(var + EPS) * lg_ref[...] + lb_ref[...]).astype(BF16)
        for g in range(groups):
            gs = slice(g * gd, (g + 1) * gd)
            s = _dot(w_masked[g], zn[:, gs]) + bs_ref[:, gs]
            o_ref[sl, gs] = (u_ref[sl, gs].astype(F32) * s).astype(o_ref.dtype)


def _sgu(uz, ln_g, ln_b, w_spatial, bs_full, seq, width):
    t = uz.shape[0]
    groups = w_spatial.shape[0]
    blk = _tile(seq, 256)
    assert blk % SGU_CHUNK == 0
    return pl.pallas_call(
        functools.partial(_sgu_kernel, n_chunks=blk // SGU_CHUNK, groups=groups),
        grid=(t // blk,),
        in_specs=[pl.BlockSpec((blk, width), lambda i: (i, 0)),
                  pl.BlockSpec((blk, width), lambda i: (i, 1)),
                  pl.BlockSpec((1, width), lambda i: (0, 0)),
                  pl.BlockSpec((1, width), lambda i: (0, 0)),
                  pl.BlockSpec((groups, SGU_CHUNK, SGU_CHUNK), lambda i: (0, 0, 0)),
                  pl.BlockSpec((SGU_CHUNK, width), lambda i: (0, 0))],
        out_specs=pl.BlockSpec((blk, width), lambda i: (i, 0)),
        out_shape=jax.ShapeDtypeStruct((t, width), BF16),
        compiler_params=_params("arbitrary"),
        name="sgu_mixer",
    )(uz, uz, ln_g, ln_b, w_spatial, bs_full)


def _merge_kernel(a_ref, b_ref, wa_ref, wb_ref, ga_ref, gb_ref, o_ref):
    ya = _dot(a_ref[...], wa_ref[...].astype(BF16))
    yb = _dot(b_ref[...], wb_ref[...].astype(BF16))
    o_ref[...] = (ga_ref[...].astype(F32) * ya + gb_ref[...].astype(F32) * yb).astype(o_ref.dtype)


def _merge(a, b, wa, wb, gates, tm, tn):
    t, val = a.shape
    width = b.shape[1]
    d = wa.shape[1]
    return pl.pallas_call(
        _merge_kernel,
        grid=(t // tm, d // tn),
        in_specs=[pl.BlockSpec((tm, val), lambda i, j: (i, 0)),
                  pl.BlockSpec((tm, width), lambda i, j: (i, 0)),
                  pl.BlockSpec((val, tn), lambda i, j: (0, j)),
                  pl.BlockSpec((width, tn), lambda i, j: (0, j)),
                  pl.BlockSpec((tm, tn), lambda i, j: (i, j)),
                  pl.BlockSpec((tm, tn), lambda i, j: (i, d // tn + j))],
        out_specs=pl.BlockSpec((tm, tn), lambda i, j: (i, j)),
        out_shape=jax.ShapeDtypeStruct((t, d), BF16),
        compiler_params=_params("arbitrary", "arbitrary"),
        name="merge_branches",
    )(a, b, wa, wb, gates, gates)


def _outproj_kernel(m_ref, w_ref, x_ref, g1_ref, o_ref):
    y = _dot(m_ref[...], w_ref[...].astype(BF16))
    o_ref[...] = x_ref[...] + g1_ref[...] * y


def _outproj(merged, w, x2d, g1, seq, tm, tn):
    t, d = x2d.shape
    bpb = seq // tm
    return pl.pallas_call(
        _outproj_kernel,
        grid=(t // tm, d // tn),
        in_specs=[pl.BlockSpec((tm, d), lambda i, j: (i, 0)),
                  pl.BlockSpec((d, tn), lambda i, j: (0, j)),
                  pl.BlockSpec((tm, tn), lambda i, j: (i, j)),
                  pl.BlockSpec((None, 1, tn), lambda i, j: (i // bpb, 0, j))],
        out_specs=pl.BlockSpec((tm, tn), lambda i, j: (i, j)),
        out_shape=jax.ShapeDtypeStruct((t, d), F32),
        compiler_params=_params("arbitrary", "arbitrary"),
        name="out_proj_residual",
    )(merged, w, x2d, g1)


def _router_kernel(x_ref, g_ref, sc_ref, sh_ref, wr_ref, br_ref,
                   hp_ref, idx_ref, wt_ref, rank_ref, cnt_ref, carry_ref, *, n_exp):
    @pl.when(pl.program_id(0) == 0)
    def _():
        carry_ref[...] = jnp.zeros_like(carry_ref)

    x = x_ref[...]
    tm = x.shape[0]
    inv = lax.rsqrt(jnp.mean(x * x, axis=-1, keepdims=True) + EPS)
    h = (x * inv) * g_ref[...] * (1.0 + sc_ref[...]) + sh_ref[...]
    hp_ref[...] = _pack_halves(h)
    h_hi = h.astype(BF16)
    h_lo = (h - h_hi.astype(F32)).astype(BF16)
    p = _dot(h_hi, wr_ref[...])
    q = _dot(h_lo, wr_ref[...])
    lane = lax.broadcasted_iota(I32, (tm, LANES), 1)
    logits = p + pltpu.roll(p, LANES - n_exp, 1) + q + br_ref[...]
    neg_inf = jnp.float32(-jnp.inf)
    work = jnp.where(lane < n_exp, logits, neg_inf)
    vals, sels = [], []
    idx_out = jnp.zeros((tm, LANES), I32)
    for k in range(TOP_K):
        m = jnp.max(work, axis=-1, keepdims=True)
        idx = jnp.min(jnp.where(work == m, lane, LANES), axis=-1, keepdims=True)
        sel = lane == idx
        work = jnp.where(sel, neg_inf, work)
        vals.append(m)
        sels.append(sel)
        idx_out = jnp.where(lane == k, idx, idx_out)
    exps = [jnp.exp(v - vals[0]) for v in vals]
    denom = exps[0]
    for e in exps[1:]:
        denom = denom + e
    wt_out = jnp.zeros((tm, LANES), F32)
    for k in range(TOP_K):
        wt_out = jnp.where(lane == k, exps[k] / denom, wt_out)
    onehot = jnp.zeros((tm, LANES), F32)
    for sel in sels:
        onehot = jnp.where(sel, 1.0, onehot)
    r = lax.broadcasted_iota(I32, (tm, tm), 0)
    c = lax.broadcasted_iota(I32, (tm, tm), 1)
    strict = jnp.where(c < r, 1.0, 0.0).astype(BF16)
    before = carry_ref[...] + _dot(strict, onehot.astype(BF16))
    rank_out = jnp.zeros((tm, LANES), I32)
    for k in range(TOP_K):
        rk = jnp.sum(jnp.where(sels[k], before, 0.0), axis=-1, keepdims=True)
        rank_out = jnp.where(lane == k, rk.astype(I32), rank_out)
    carry_ref[...] = carry_ref[...] + jnp.sum(onehot, axis=0, keepdims=True)
    idx_ref[...] = idx_out
    wt_ref[...] = wt_out
    rank_ref[...] = rank_out
    cnt_ref[...] = carry_ref[...]


def _router(x1, g, sc, sh, wr_cat, br_pad, seq, n_exp):
    t, d = x1.shape
    tm = _tile(seq, 256)
    bpb = seq // tm
    tok = pl.BlockSpec((tm, LANES), lambda i: (i, 0))
    return pl.pallas_call(
        functools.partial(_router_kernel, n_exp=n_exp),
        grid=(t // tm,),
        in_specs=[pl.BlockSpec((tm, d), lambda i: (i, 0)),
                  pl.BlockSpec((1, d), lambda i: (0, 0)),
                  pl.BlockSpec((None, 1, d), lambda i: (i // bpb, 0, 0)),
                  pl.BlockSpec((None, 1, d), lambda i: (i // bpb, 0, 0)),
                  pl.BlockSpec((d, LANES), lambda i: (0, 0)),
                  pl.BlockSpec((1, LANES), lambda i: (0, 0))],
        out_specs=[pl.BlockSpec((tm, d // 2), lambda i: (i, 0)), tok, tok, tok,
                   pl.BlockSpec((1, LANES), lambda i: (0, 0))],
        out_shape=[jax.ShapeDtypeStruct((t, d // 2), U32),
                   jax.ShapeDtypeStruct((t, LANES), I32),
                   jax.ShapeDtypeStruct((t, LANES), F32),
                   jax.ShapeDtypeStruct((t, LANES), I32),
                   jax.ShapeDtypeStruct((1, LANES), F32)],
        scratch_shapes=[pltpu.VMEM((1, LANES), F32)],
        compiler_params=_params("arbitrary"),
        name="norm_router_topk",
    )(x1, g, sc, sh, wr_cat, br_pad)


def _dispatch_kernel(pad_end_ref, padded_ref, pos_ref, hp_ref, xs_ref, zero_ref, zsem, sem, *, tm, n_exp):
    @pl.when(pl.program_id(0) == 0)
    def _():
        zero_ref[...] = jnp.zeros_like(zero_ref)

        def zero_block(b):
            start = pl.multiple_of(b * MOE_ROWS, MOE_ROWS)
            return pltpu.make_async_copy(zero_ref, xs_ref.at[pl.ds(start, MOE_ROWS)], zsem)

        def start_tail(b, carry):
            zero_block(b).start()
            return carry

        def wait_tail(b, carry):
            zero_block(b).wait()
            return carry

        n_used = pad_end_ref[n_exp - 1] // MOE_ROWS
        n_blocks = xs_ref.shape[0] // MOE_ROWS
        for e in range(n_exp):
            @pl.when(padded_ref[e] > 0)
            def _():
                zero_block(pad_end_ref[e] // MOE_ROWS - 1).start()
        lax.fori_loop(n_used, n_blocks, start_tail, 0)
        for e in range(n_exp):
            @pl.when(padded_ref[e] > 0)
            def _():
                zero_block(pad_end_ref[e] // MOE_ROWS - 1).wait()
        lax.fori_loop(n_used, n_blocks, wait_tail, 0)

    def row_copy(r, p):
        return pltpu.make_async_copy(hp_ref.at[pl.ds(r, 1)], xs_ref.at[pl.ds(p, 1)], sem)

    def start(r, carry):
        for k in range(TOP_K):
            row_copy(r, pos_ref[r * TOP_K + k]).start(priority=k % 2)
        return carry

    def wait(r, carry):
        for k in range(TOP_K):
            row_copy(r, pos_ref[r * TOP_K + k]).wait()
        return carry

    lax.fori_loop(0, tm, start, 0)
    lax.fori_loop(0, tm, wait, 0)


def _dispatch(pad_end, padded, pos_flat, hp, n_rows):
    t, half = hp.shape
    tm = _tile(t, 256)
    n_exp = pad_end.shape[0]
    grid_spec = pltpu.PrefetchScalarGridSpec(
        num_scalar_prefetch=2,
        grid=(t // tm,),
        in_specs=[pl.BlockSpec((tm * TOP_K,), lambda i, pe, pd: (i,), memory_space=pltpu.SMEM),
                  pl.BlockSpec((tm, half), lambda i, pe, pd: (i, 0))],
        out_specs=pl.BlockSpec(memory_space=pl.ANY),
        scratch_shapes=[pltpu.VMEM((MOE_ROWS, half), U32),
                        pltpu.SemaphoreType.DMA(()), pltpu.SemaphoreType.DMA(())],
    )
    return pl.pallas_call(
        functools.partial(_dispatch_kernel, tm=tm, n_exp=n_exp),
        grid_spec=grid_spec,
        out_shape=jax.ShapeDtypeStruct((n_rows, half), U32),
        compiler_params=_params("arbitrary"),
        name="moe_dispatch",
    )(pad_end, padded, pos_flat, hp)


def _weight_runs(blk_exp, n_outer):
    n_blocks = blk_exp.shape[0]
    total = n_outer * n_blocks
    e_flat = jnp.tile(blk_exp, n_outer)
    j_flat = jnp.repeat(jnp.arange(n_outer, dtype=I32), n_blocks)
    changed = (e_flat[1:] != e_flat[:-1]) | (j_flat[1:] != j_flat[:-1])
    first = jnp.concatenate([jnp.ones((1,), bool), changed])
    steps = jnp.arange(total, dtype=I32)
    nxt = lax.cummin(jnp.where(first, steps, total)[::-1])[::-1]
    nxt = jnp.concatenate([nxt[1:], jnp.full((1,), total, I32)])
    has_next = nxt < total
    nxt_c = jnp.minimum(nxt, total - 1)
    return (first.astype(I32), has_next.astype(I32), e_flat[nxt_c].astype(I32), j_flat[nxt_c].astype(I32))


def _expert_up_kernel(be_ref, nv_ref, first_ref, pf_ok_ref, pf_e_ref, pf_j_ref,
                      x_ref, bg_ref, bu_ref, wg_hbm, wu_hbm, h_ref, stage_ref, wbf_ref, sem, *, tf):
    j, i = pl.program_id(0), pl.program_id(1)
    s = j * pl.num_programs(1) + i

    def w_copies(e, jj):
        cols = pl.ds(pl.multiple_of(jj * tf, tf), tf)
        return [pltpu.make_async_copy(w.at[0, e, :, cols], stage_ref.at[t], sem.at[t])
                for t, w in enumerate((wg_hbm, wu_hbm))]

    @pl.when(s == 0)
    def _():
        for cp in w_copies(be_ref[0], 0):
            cp.start()

    @pl.when(first_ref[s] == 1)
    def _():
        for cp in w_copies(be_ref[i], j):
            cp.wait()
        wbf_ref[0] = stage_ref[0].astype(BF16)
        wbf_ref[1] = stage_ref[1].astype(BF16)

        @pl.when(pf_ok_ref[s] == 1)
        def _():
            for cp in w_copies(pf_e_ref[s], pf_j_ref[s]):
                cp.start()

    valid = i < nv_ref[0]

    @pl.when(valid)
    def _():
        xp = x_ref[...]
        half = xp.shape[1]
        x_lo = _unpack_lo(xp).astype(BF16)
        x_hi = _unpack_hi(xp).astype(BF16)
        glu = _dot(x_lo, wbf_ref[0, :half, :]) + _dot(x_hi, wbf_ref[0, half:, :]) + bg_ref[...]
        lin = _dot(x_lo, wbf_ref[1, :half, :]) + _dot(x_hi, wbf_ref[1, half:, :]) + bu_ref[...]
        glu = jnp.minimum(glu, SWIGLU_LIMIT)
        lin = jnp.clip(lin, -SWIGLU_LIMIT, SWIGLU_LIMIT)
        h_ref[...] = (glu * _sigmoid(SWIGLU_ALPHA * glu) * (lin + 1.0)).astype(h_ref.dtype)

    @pl.when(jnp.logical_not(valid))
    def _():
        h_ref[...] = jnp.zeros_like(h_ref)


def _expert_up(blk_exp, n_valid, xs, wg, bg, wu, bu, tf):
    n_rows, half = xs.shape
    d, f = wg.shape[-2], wg.shape[-1]
    n_blocks = n_rows // MOE_ROWS
    first, pf_ok, pf_e, pf_j = _weight_runs(blk_exp, f // tf)
    last = lambda i, nv: jnp.minimum(i, nv[0] - 1)
    grid_spec = pltpu.PrefetchScalarGridSpec(
        num_scalar_prefetch=6,
        grid=(f // tf, n_blocks),
        in_specs=[pl.BlockSpec((MOE_ROWS, half), lambda j, i, be, nv, *_: (last(i, nv), 0)),
                  pl.BlockSpec((None, 1, tf), lambda j, i, be, nv, *_: (be[i], 0, j)),
                  pl.BlockSpec((None, 1, tf), lambda j, i, be, nv, *_: (be[i], 0, j)),
                  pl.BlockSpec(memory_space=pl.ANY),
                  pl.BlockSpec(memory_space=pl.ANY)],
        out_specs=pl.BlockSpec((MOE_ROWS, tf), lambda j, i, be, nv, *_: (i, j)),
        scratch_shapes=[pltpu.VMEM((2, d, tf), F32), pltpu.VMEM((2, d, tf), BF16),
                        pltpu.SemaphoreType.DMA((2,))],
    )
    return pl.pallas_call(
        functools.partial(_expert_up_kernel, tf=tf),
        grid_spec=grid_spec,
        out_shape=jax.ShapeDtypeStruct((n_rows, f), BF16),
        compiler_params=_params("arbitrary", "arbitrary"),
        name="expert_up",
    )(blk_exp, n_valid, first, pf_ok, pf_e, pf_j, xs, bg, bu, wg, wu)


def _expert_down_kernel(be_ref, nv_ref, first_ref, pf_ok_ref, pf_e_ref, pf_j_ref,
                        h_ref, bd_ref, wd_hbm, o_ref, stage_ref, wbf_ref, sem, *, tn):
    j, i = pl.program_id(0), pl.program_id(1)
    s = j * pl.num_programs(1) + i

    def w_copy(e, jj):
        cols = pl.ds(pl.multiple_of(jj * tn, tn), tn)
        return pltpu.make_async_copy(wd_hbm.at[0, e, :, cols], stage_ref, sem)

    @pl.when(s == 0)
    def _():
        w_copy(be_ref[0], 0).start()

    @pl.when(first_ref[s] == 1)
    def _():
        w_copy(be_ref[i], j).wait()
        wbf_ref[...] = stage_ref[...].astype(BF16)

        @pl.when(pf_ok_ref[s] == 1)
        def _():
            w_copy(pf_e_ref[s], pf_j_ref[s]).start()

    valid = i < nv_ref[0]

    @pl.when(valid)
    def _():
        y = _dot(h_ref[...], wbf_ref[...]) + bd_ref[...]
        o_ref[...] = _pack_halves(y)

    @pl.when(jnp.logical_not(valid))
    def _():
        o_ref[...] = jnp.zeros_like(o_ref)


def _expert_down(blk_exp, n_valid, hid, wd, bd, tn):
    n_rows, f = hid.shape
    d = wd.shape[-1]
    n_blocks = n_rows // MOE_ROWS
    first, pf_ok, pf_e, pf_j = _weight_runs(blk_exp, d // tn)
    last = lambda i, nv: jnp.minimum(i, nv[0] - 1)
    grid_spec = pltpu.PrefetchScalarGridSpec(
        num_scalar_prefetch=6,
        grid=(d // tn, n_blocks),
        in_specs=[pl.BlockSpec((MOE_ROWS, f), lambda j, i, be, nv, *_: (last(i, nv), 0)),
                  pl.BlockSpec((None, 1, tn), lambda j, i, be, nv, *_: (be[i], 0, j)),
                  pl.BlockSpec(memory_space=pl.ANY)],
        out_specs=pl.BlockSpec((MOE_ROWS, tn // 2), lambda j, i, be, nv, *_: (i, j)),
        scratch_shapes=[pltpu.VMEM((f, tn), F32), pltpu.VMEM((f, tn), BF16), pltpu.SemaphoreType.DMA(())],
    )
    return pl.pallas_call(
        functools.partial(_expert_down_kernel, tn=tn),
        grid_spec=grid_spec,
        out_shape=jax.ShapeDtypeStruct((n_rows, d // 2), U32),
        compiler_params=_params("arbitrary", "arbitrary"),
        name="expert_down",
    )(blk_exp, n_valid, first, pf_ok, pf_e, pf_j, hid, bd, wd)


def _combine_kernel(pos_ref, pos_next_ref, wt_ref, x_ref, g2_ref, gf_ref, ys_ref, o_ref, buf_ref, sem,
                    *, tm, tn):
    i = pl.program_id(0)
    slot = i % 2

    def row_copy(sl, r, k, p):
        return pltpu.make_async_copy(ys_ref.at[pl.ds(p, 1)], buf_ref.at[sl, k, pl.ds(r, 1)], sem.at[sl])

    def issue(p_ref, sl):
        def body(r, carry):
            for k in range(TOP_K):
                row_copy(sl, r, k, p_ref[r * TOP_K + k]).start(priority=k % 2)
            return carry
        lax.fori_loop(0, tm, body, 0)

    @pl.when(i == 0)
    def _():
        issue(pos_ref, 0)

    @pl.when(i + 1 < pl.num_programs(0))
    def _():
        issue(pos_next_ref, 1 - slot)

    def wait(r, carry):
        for k in range(TOP_K):
            row_copy(slot, r, k, 0).wait()
        return carry

    lax.fori_loop(0, tm, wait, 0)

    d = x_ref.shape[1]
    hw = tn // 2
    wts = [wt_ref[:, k:k + 1] for k in range(TOP_K)]
    pieces = []
    ssq = jnp.zeros((tm, 1), F32)
    for j in range(d // tn):
        for part, unpack in ((0, _unpack_lo), (1, _unpack_hi)):
            cols = slice(j * tn + part * hw, j * tn + (part + 1) * hw)
            y = jnp.zeros((tm, hw), F32)
            for k in range(TOP_K):
                y = y + wts[k] * unpack(buf_ref[slot, k, :, j * hw:(j + 1) * hw])
            x2 = x_ref[:, cols] + g2_ref[:, cols] * y
            ssq = ssq + jnp.sum(x2 * x2, axis=-1, keepdims=True)
            pieces.append((cols, x2))
    inv = lax.rsqrt(ssq / d + EPS)
    for cols, x2 in pieces:
        o_ref[:, cols] = x2 * inv * gf_ref[:, cols]


def _combine(pos_flat, wt, x1, g2, gf, ys, seq, tn):
    t, d = x1.shape
    tm = _tile(seq, 128)
    bpb = seq // tm
    n_steps = t // tm
    return pl.pallas_call(
        functools.partial(_combine_kernel, tm=tm, tn=tn),
        grid=(n_steps,),
        in_specs=[pl.BlockSpec((tm * TOP_K,), lambda i: (i,), memory_space=pltpu.SMEM),
                  pl.BlockSpec((tm * TOP_K,), lambda i: (jnp.minimum(i + 1, n_steps - 1),),
                               memory_space=pltpu.SMEM),
                  pl.BlockSpec((tm, LANES), lambda i: (i, 0)),
                  pl.BlockSpec((tm, d), lambda i: (i, 0)),
                  pl.BlockSpec((None, 1, d), lambda i: (i // bpb, 0, 0)),
                  pl.BlockSpec((1, d), lambda i: (0, 0)),
                  pl.BlockSpec(memory_space=pl.ANY)],
        out_specs=pl.BlockSpec((tm, d), lambda i: (i, 0)),
        out_shape=jax.ShapeDtypeStruct((t, d), F32),
        scratch_shapes=[pltpu.VMEM((2, TOP_K, tm, d // 2), U32), pltpu.SemaphoreType.DMA((2,))],
        compiler_params=_params("arbitrary"),
        name="moe_combine_final_norm",
    )(pos_flat, pos_flat, wt, x1, g2, gf, ys)


def _layer(x2d, c_pad, batch, seq, w_ada, b_ada, norm_mix_g, w_in, w_alpha_up, b_alpha, gla_norm_g, sgu_ln_g,
           sgu_ln_b, w_spatial, b_spatial, w_branch_a, w_branch_b, w_out, norm_ffn_g, w_router, b_router,
           w_exp_gate, b_exp_gate, w_exp_up, b_exp_up, w_exp_down, b_exp_down, final_g):
    t, d = x2d.shape
    rank, key = w_alpha_up.shape
    val = w_branch_a.shape[0]
    width = w_branch_b.shape[0]
    n_exp = w_router.shape[1]
    f = w_exp_gate.shape[-1]
    assert 2 * n_exp <= LANES and rank <= LANES

    mod = _ada(c_pad, w_ada, b_ada.reshape(1, -1))[:batch].reshape(batch, 6, 1, d)
    sh1, sc1, g1, sh2, sc2, g2 = (mod[:, i] for i in range(6))

    h = _norm_mod(x2d, norm_mix_g.reshape(1, d), sc1, sh1, seq)

    tm = _tile(t, 1024)
    n_qkvg = 2 * key + 2 * val
    qkvg = _matmul_act(h, w_in, tm, _tile(key, 512), 0, n_qkvg, None, "proj_qkvg")
    walr = jnp.pad(w_in[:, n_qkvg:n_qkvg + rank], ((0, 0), (0, LANES - rank))).astype(BF16)
    wup = jnp.pad(w_alpha_up, ((0, LANES - rank), (0, 0))).astype(BF16)
    log_a = _alpha(h, walr, wup, b_alpha.reshape(1, key))
    w_rest = w_in[:, n_qkvg + rank:].astype(BF16)
    tn_w = _tile(width, 512)
    uz = _matmul_act(h, w_rest, tm, tn_w, 0, 2 * width, "gelu", "proj_uz")
    gates = _matmul_act(h, w_rest, tm, tn_w, 2 * width, 2 * d, "sigmoid", "proj_gates")

    a = _gla(qkvg, log_a, gla_norm_g.reshape(1, -1), batch, seq, key, val)
    bs_full = jnp.repeat(b_spatial.T, width // w_spatial.shape[0], axis=1)
    b = _sgu(uz, sgu_ln_g.reshape(1, width), sgu_ln_b.reshape(1, width), w_spatial, bs_full, seq, width)

    tn = _tile(d, 512)
    merged = _merge(a, b, w_branch_a, w_branch_b, gates, tm, tn)
    x1 = _outproj(merged, w_out, x2d, g1, seq, _tile(seq, 1024), tn)

    w_hi = w_router.astype(BF16)
    w_lo = (w_router - w_hi.astype(F32)).astype(BF16)
    wr_cat = jnp.pad(jnp.concatenate([w_hi, w_lo], axis=1), ((0, 0), (0, LANES - 2 * n_exp)))
    br_pad = jnp.pad(b_router.reshape(1, n_exp), ((0, 0), (0, LANES - n_exp)))
    hp, top_idx, top_w, rank_in_exp, counts = _router(x1, norm_ffn_g.reshape(1, d), sc2, sh2, wr_cat, br_pad,
                                                      seq, n_exp)

    counts = counts[0, :n_exp].astype(I32)
    padded = (counts + MOE_ROWS - 1) // MOE_ROWS * MOE_ROWS
    pad_end = jnp.cumsum(padded)
    pad_start = pad_end - padded
    top_idx = top_idx[:, :TOP_K]
    pos_flat = (pad_start[top_idx] + rank_in_exp[:, :TOP_K]).reshape(-1)
    n_rows = t * TOP_K + n_exp * MOE_ROWS
    n_blocks = n_rows // MOE_ROWS
    blk_start = jnp.arange(n_blocks, dtype=I32) * MOE_ROWS
    blk_exp = jnp.minimum(jnp.sum(blk_start[:, None] >= pad_end[None, :], axis=1), n_exp - 1).astype(I32)
    n_valid = (pad_end[-1:] // MOE_ROWS).astype(I32)

    xs = _dispatch(pad_end.astype(I32), padded.astype(I32), pos_flat, hp, n_rows)
    hid = _expert_up(blk_exp, n_valid, xs, w_exp_gate, b_exp_gate.reshape(n_exp, 1, f),
                     w_exp_up, b_exp_up.reshape(n_exp, 1, f), _tile(f, 512))
    tn_down = _tile(d, 2048)
    ys = _expert_down(blk_exp, n_valid, hid, w_exp_down, b_exp_down.reshape(n_exp, 1, d), tn_down)
    return _combine(pos_flat, top_w, x1, g2, final_g, ys, seq, tn_down)


def kernel(x, c, w_ada, b_ada, norm_mix_g, w_in, w_alpha_up, b_alpha, gla_norm_g, sgu_ln_g, sgu_ln_b,
           w_spatial, b_spatial, w_branch_a, w_branch_b, w_out, norm_ffn_g, w_router, b_router,
           w_exp_gate, b_exp_gate, w_exp_up, b_exp_up, w_exp_down, b_exp_down, norm_final_g):
    batch, seq, d = x.shape
    depth = w_ada.shape[0]
    assert depth == 1, "the final rmsnorm is fused into the last layer's combine kernel"
    c_pad = jnp.pad(c, ((0, 8 - batch % 8), (0, 0))) if batch % 8 else c
    out = _layer(x.reshape(batch * seq, d), c_pad, batch, seq, w_ada[0], b_ada[0], norm_mix_g[0], w_in[0],
                 w_alpha_up[0], b_alpha[0], gla_norm_g[0], sgu_ln_g[0], sgu_ln_b[0], w_spatial[0],
                 b_spatial[0], w_branch_a[0], w_branch_b[0], w_out[0], norm_ffn_g[0], w_router[0],
                 b_router[0], w_exp_gate, b_exp_gate[0], w_exp_up, b_exp_up[0], w_exp_down, b_exp_down[0],
                 norm_final_g.reshape(1, d))
    return out.reshape(batch, seq, d)
```

```python
import functools

import jax
import jax.numpy as jnp
from jax import lax
from jax.experimental import pallas as pl
from jax.experimental.pallas import tpu as pltpu

GLA_HEADS = 4
GLA_CHUNK = 64
GLA_TAU = 16.0
SGU_CHUNK = 128
TOP_K = 4
SWIGLU_ALPHA = 1.702
SWIGLU_LIMIT = 7.0
EPS = 1e-5

LANES = 128
MOE_ROWS = 512
VMEM_LIMIT = 56 * 1024 * 1024

F32 = jnp.float32
BF16 = jnp.bfloat16
U32 = jnp.uint32
I32 = jnp.int32


def _params(*sem):
    return pltpu.CompilerParams(dimension_semantics=sem, vmem_limit_bytes=VMEM_LIMIT)


def _tile(n, want):
    if n <= want:
        return n
    t = want
    while t >= LANES:
        if n % t == 0:
            return t
        t -= LANES
    return n


def _dot(a, b):
    return jnp.dot(a, b, preferred_element_type=F32)


def _dot_nt(a, b):
    return lax.dot_general(a, b, (((1,), (1,)), ((), ())), preferred_element_type=F32)


def _dot_tn(a, b):
    return lax.dot_general(a, b, (((0,), (0,)), ((), ())), preferred_element_type=F32)


def _sigmoid(x):
    return 1.0 / (1.0 + jnp.exp(-x))


def _pack_halves(y):
    half = y.shape[1] // 2
    bits = lax.bitcast_convert_type(y.astype(BF16).astype(F32), U32)
    return (bits[:, :half] >> 16) | bits[:, half:]


def _unpack_lo(p):
    return lax.bitcast_convert_type(p << 16, F32)


def _unpack_hi(p):
    return lax.bitcast_convert_type(p & jnp.uint32(0xFFFF0000), F32)


def _ada_kernel(c_ref, w_ref, b_ref, o_ref):
    c = c_ref[...]
    s = c * _sigmoid(c)
    o_ref[...] = _dot(s.astype(BF16), w_ref[...].astype(BF16)) + b_ref[...]


def _ada(c_pad, w, b):
    rows, d = c_pad.shape
    n = w.shape[1]
    tn = _tile(n, 512)
    return pl.pallas_call(
        _ada_kernel,
        grid=(n // tn,),
        in_specs=[pl.BlockSpec((rows, d), lambda j: (0, 0)),
                  pl.BlockSpec((d, tn), lambda j: (0, j)),
                  pl.BlockSpec((1, tn), lambda j: (0, j))],
        out_specs=pl.BlockSpec((rows, tn), lambda j: (0, j)),
        out_shape=jax.ShapeDtypeStruct((rows, n), F32),
        compiler_params=_params("arbitrary"),
        name="ada_mod",
    )(c_pad, w, b)


def _norm_mod_kernel(x_ref, g_ref, sc_ref, sh_ref, o_ref):
    x = x_ref[...]
    inv = lax.rsqrt(jnp.mean(x * x, axis=-1, keepdims=True) + EPS)
    h = (x * inv) * g_ref[...] * (1.0 + sc_ref[...]) + sh_ref[...]
    o_ref[...] = h.astype(o_ref.dtype)


def _norm_mod(x2d, g, sc, sh, seq):
    t, d = x2d.shape
    tm = _tile(seq, 256)
    bpb = seq // tm
    return pl.pallas_call(
        _norm_mod_kernel,
        grid=(t // tm,),
        in_specs=[pl.BlockSpec((tm, d), lambda i: (i, 0)),
                  pl.BlockSpec((1, d), lambda i: (0, 0)),
                  pl.BlockSpec((None, 1, d), lambda i: (i // bpb, 0, 0)),
                  pl.BlockSpec((None, 1, d), lambda i: (i // bpb, 0, 0))],
        out_specs=pl.BlockSpec((tm, d), lambda i: (i, 0)),
        out_shape=jax.ShapeDtypeStruct((t, d), BF16),
        compiler_params=_params("arbitrary"),
        name="norm_mod",
    )(x2d, g, sc, sh)


def _mm_act_kernel(a_ref, wt_ref, o_ref, *, act):
    acc = _dot_nt(a_ref[...], wt_ref[...].astype(BF16))
    if act == "gelu":
        acc = jax.nn.gelu(acc)
    elif act == "sigmoid":
        acc = _sigmoid(acc)
    o_ref[...] = acc.astype(o_ref.dtype)


def _matmul_act(a, w_t, tm, tn, row0, n_rows, act, name):
    m, k = a.shape
    assert n_rows % tn == 0 and row0 % 8 == 0
    return pl.pallas_call(
        functools.partial(_mm_act_kernel, act=act),
        grid=(m // tm, n_rows // tn),
        in_specs=[pl.BlockSpec((tm, k), lambda i, j: (i, 0)),
                  pl.BlockSpec((pl.Element(tn), pl.Element(k)),
                               lambda i, j: (pl.multiple_of(row0 + j * tn, 8), 0))],
        out_specs=pl.BlockSpec((tm, tn), lambda i, j: (i, j)),
        out_shape=jax.ShapeDtypeStruct((m, n_rows), BF16),
        compiler_params=_params("arbitrary", "arbitrary"),
        name=name,
    )(a, w_t)


def _alpha_kernel(h_ref, wt_ref, wup_ref, b_ref, o_ref, *, rank):
    a = _dot_nt(h_ref[...], wt_ref[...].astype(BF16))
    lane = lax.broadcasted_iota(I32, a.shape, 1)
    a = jnp.where(lane < rank, a, 0.0)
    z = _dot(a.astype(BF16), wup_ref[...]) + b_ref[...]
    log_sig = jnp.minimum(z, 0.0) - jnp.log1p(jnp.exp(-jnp.abs(z)))
    o_ref[...] = log_sig / GLA_TAU


def _alpha(h, w_t, row0, rank, wup_pad, b_alpha):
    t, d = h.shape
    key = wup_pad.shape[1]
    tm = _tile(t, 1024)
    assert row0 % 8 == 0 and row0 + LANES <= w_t.shape[0]
    return pl.pallas_call(
        functools.partial(_alpha_kernel, rank=rank),
        grid=(t // tm,),
        in_specs=[pl.BlockSpec((tm, d), lambda i: (i, 0)),
                  pl.BlockSpec((pl.Element(LANES), pl.Element(d)), lambda i: (row0, 0)),
                  pl.BlockSpec((LANES, key), lambda i: (0, 0)),
                  pl.BlockSpec((1, key), lambda i: (0, 0))],
        out_specs=pl.BlockSpec((tm, key), lambda i: (i, 0)),
        out_shape=jax.ShapeDtypeStruct((t, key), F32),
        compiler_params=_params("arbitrary"),
        name="gla_log_alpha",
    )(h, w_t, wup_pad, b_alpha)


def _gla_kernel(q_ref, k_ref, v_ref, la_ref, g_ref, gn_ref, o_ref, st_ref, *, n_chunks, scale):
    @pl.when(pl.program_id(1) == 0)
    def _():
        st_ref[...] = jnp.zeros_like(st_ref)

    c_len = GLA_CHUNK
    dk = q_ref.shape[1] // GLA_HEADS
    dv = v_ref.shape[1] // GLA_HEADS
    row = lax.broadcasted_iota(I32, (c_len, c_len), 0)
    col = lax.broadcasted_iota(I32, (c_len, c_len), 1)
    causal = col <= row
    tri = jnp.where(causal, 1.0, 0.0).astype(BF16)
    for c in range(n_chunks):
        sl = slice(c * c_len, (c + 1) * c_len)
        for hd in range(GLA_HEADS):
            ks = slice(hd * dk, (hd + 1) * dk)
            vs = slice(hd * dv, (hd + 1) * dv)
            la = la_ref[sl, ks]
            la_hi = la.astype(BF16)
            la_lo = (la - la_hi.astype(F32)).astype(BF16)
            cum = _dot(tri, la_hi) + _dot(tri, la_lo)
            cum_last = cum[c_len - 1:c_len, :]
            q = q_ref[sl, ks].astype(F32) * scale
            k = k_ref[sl, ks].astype(F32)
            v = v_ref[sl, vs]
            q_abs = (q * jnp.exp(cum)).astype(BF16)
            q_rel = (q * jnp.exp(cum - cum_last)).astype(BF16)
            k_rel = (k * jnp.exp(cum_last - cum)).astype(BF16)
            decay = jnp.exp(cum_last)
            scores = jnp.where(causal, _dot_nt(q_rel, k_rel), 0.0).astype(BF16)
            st = st_ref[hd]
            o = _dot(scores, v) + _dot_nt(q_abs, st.astype(BF16))
            st_ref[hd] = st * decay + _dot_tn(v, k_rel)
            inv = lax.rsqrt(jnp.mean(o * o, axis=-1, keepdims=True) + EPS)
            g = g_ref[sl, vs].astype(F32)
            o_ref[sl, vs] = (o * inv * gn_ref[...] * (g * _sigmoid(g))).astype(o_ref.dtype)


def _gla(qkvg, log_a, gn, batch, seq, key, val):
    t = qkvg.shape[0]
    dk, dv = key // GLA_HEADS, val // GLA_HEADS
    blk = _tile(seq, 128)
    assert blk % GLA_CHUNK == 0 and (2 * key) % val == 0
    nb = seq // blk
    row = lambda b, n: b * nb + n
    return pl.pallas_call(
        functools.partial(_gla_kernel, n_chunks=blk // GLA_CHUNK, scale=float(dk) ** -0.5),
        grid=(batch, nb),
        in_specs=[pl.BlockSpec((blk, key), lambda b, n: (row(b, n), 0)),
                  pl.BlockSpec((blk, key), lambda b, n: (row(b, n), 1)),
                  pl.BlockSpec((blk, val), lambda b, n: (row(b, n), 2 * key // val)),
                  pl.BlockSpec((blk, key), lambda b, n: (row(b, n), 0)),
                  pl.BlockSpec((blk, val), lambda b, n: (row(b, n), 2 * key // val + 1)),
                  pl.BlockSpec((1, dv), lambda b, n: (0, 0))],
        out_specs=pl.BlockSpec((blk, val), lambda b, n: (row(b, n), 0)),
        out_shape=jax.ShapeDtypeStruct((t, val), BF16),
        scratch_shapes=[pltpu.VMEM((GLA_HEADS, dv, dk), F32)],
        compiler_params=_params("arbitrary", "arbitrary"),
        name="gla_mixer",
    )(qkvg, qkvg, qkvg, log_a, qkvg, gn)


def _sgu_kernel(u_ref, z_ref, lg_ref, lb_ref, w_ref, bs_ref, o_ref, *, n_chunks, groups):
    c_len = SGU_CHUNK
    gd = u_ref.shape[1] // groups
    row = lax.broadcasted_iota(I32, (c_len, c_len), 0)
    col = lax.broadcasted_iota(I32, (c_len, c_len), 1)
    causal = col <= row
    w_masked = [jnp.where(causal, w_ref[g], 0.0).astype(BF16) for g in range(groups)]
    for c in range(n_chunks):
        sl = slice(c * c_len, (c + 1) * c_len)
        z = z_ref[sl, :].astype(F32)
        mu = jnp.mean(z, axis=-1, keepdims=True)
        zc = z - mu
        var = jnp.mean(zc * zc, axis=-1, keepdims=True)
        zn = (zc * lax.rsqrt(var + EPS) * lg_ref[...] + lb_ref[...]).astype(BF16)
        for g in range(groups):
            gs = slice(g * gd, (g + 1) * gd)
            s = _dot(w_masked[g], zn[:, gs]) + bs_ref[:, gs]
            o_ref[sl, gs] = (u_ref[sl, gs].astype(F32) * s).astype(o_ref.dtype)


def _sgu(uz, ln_g, ln_b, w_spatial, bs_full, seq, width):
    t = uz.shape[0]
    groups = w_spatial.shape[0]
    blk = _tile(seq, 256)
    assert blk % SGU_CHUNK == 0
    return pl.pallas_call(
        functools.partial(_sgu_kernel, n_chunks=blk // SGU_CHUNK, groups=groups),
        grid=(t // blk,),
        in_specs=[pl.BlockSpec((blk, width), lambda i: (i, 0)),
                  pl.BlockSpec((blk, width), lambda i: (i, 1)),
                  pl.BlockSpec((1, width), lambda i: (0, 0)),
                  pl.BlockSpec((1, width), lambda i: (0, 0)),
                  pl.BlockSpec((groups, SGU_CHUNK, SGU_CHUNK), lambda i: (0, 0, 0)),
                  pl.BlockSpec((SGU_CHUNK, width), lambda i: (0, 0))],
        out_specs=pl.BlockSpec((blk, width), lambda i: (i, 0)),
        out_shape=jax.ShapeDtypeStruct((t, width), BF16),
        compiler_params=_params("arbitrary"),
        name="sgu_mixer",
    )(uz, uz, ln_g, ln_b, w_spatial, bs_full)


def _merge_kernel(a_ref, b_ref, wa_ref, wb_ref, ga_ref, gb_ref, o_ref):
    ya = _dot(a_ref[...], wa_ref[...].astype(BF16))
    yb = _dot(b_ref[...], wb_ref[...].astype(BF16))
    o_ref[...] = (ga_ref[...].astype(F32) * ya + gb_ref[...].astype(F32) * yb).astype(o_ref.dtype)


def _merge(a, b, wa, wb, gates, tm, tn):
    t, val = a.shape
    width = b.shape[1]
    d = wa.shape[1]
    return pl.pallas_call(
        _merge_kernel,
        grid=(t // tm, d // tn),
        in_specs=[pl.BlockSpec((tm, val), lambda i, j: (i, 0)),
                  pl.BlockSpec((tm, width), lambda i, j: (i, 0)),
                  pl.BlockSpec((val, tn), lambda i, j: (0, j)),
                  pl.BlockSpec((width, tn), lambda i, j: (0, j)),
                  pl.BlockSpec((tm, tn), lambda i, j: (i, j)),
                  pl.BlockSpec((tm, tn), lambda i, j: (i, d // tn + j))],
        out_specs=pl.BlockSpec((tm, tn), lambda i, j: (i, j)),
        out_shape=jax.ShapeDtypeStruct((t, d), BF16),
        compiler_params=_params("arbitrary", "arbitrary"),
        name="merge_branches",
    )(a, b, wa, wb, gates, gates)


def _outproj_kernel(m_ref, w_ref, x_ref, g1_ref, o_ref):
    y = _dot(m_ref[...], w_ref[...].astype(BF16))
    o_ref[...] = x_ref[...] + g1_ref[...] * y


def _outproj(merged, w, x2d, g1, seq, tm, tn):
    t, d = x2d.shape
    bpb = seq // tm
    return pl.pallas_call(
        _outproj_kernel,
        grid=(t // tm, d // tn),
        in_specs=[pl.BlockSpec((tm, d), lambda i, j: (i, 0)),
                  pl.BlockSpec((d, tn), lambda i, j: (0, j)),
                  pl.BlockSpec((tm, tn), lambda i, j: (i, j)),
                  pl.BlockSpec((None, 1, tn), lambda i, j: (i // bpb, 0, j))],
        out_specs=pl.BlockSpec((tm, tn), lambda i, j: (i, j)),
        out_shape=jax.ShapeDtypeStruct((t, d), F32),
        compiler_params=_params("arbitrary", "arbitrary"),
        name="out_proj_residual",
    )(merged, w, x2d, g1)


def _router_kernel(x_ref, g_ref, sc_ref, sh_ref, wr_ref, br_ref,
                   hp_ref, idx_ref, wt_ref, rank_ref, cnt_ref, carry_ref, *, n_exp):
    @pl.when(pl.program_id(0) == 0)
    def _():
        carry_ref[...] = jnp.zeros_like(carry_ref)

    x = x_ref[...]
    tm = x.shape[0]
    inv = lax.rsqrt(jnp.mean(x * x, axis=-1, keepdims=True) + EPS)
    h = (x * inv) * g_ref[...] * (1.0 + sc_ref[...]) + sh_ref[...]
    hp_ref[...] = _pack_halves(h)
    h_hi = h.astype(BF16)
    h_lo = (h - h_hi.astype(F32)).astype(BF16)
    p = _dot(h_hi, wr_ref[...])
    q = _dot(h_lo, wr_ref[...])
    lane = lax.broadcasted_iota(I32, (tm, LANES), 1)
    logits = p + pltpu.roll(p, LANES - n_exp, 1) + q + br_ref[...]
    neg_inf = jnp.float32(-jnp.inf)
    work = jnp.where(lane < n_exp, logits, neg_inf)
    vals, sels = [], []
    idx_out = jnp.zeros((tm, LANES), I32)
    for k in range(TOP_K):
        m = jnp.max(work, axis=-1, keepdims=True)
        idx = jnp.min(jnp.where(work == m, lane, LANES), axis=-1, keepdims=True)
        sel = lane == idx
        work = jnp.where(sel, neg_inf, work)
        vals.append(m)
        sels.append(sel)
        idx_out = jnp.where(lane == k, idx, idx_out)
    exps = [jnp.exp(v - vals[0]) for v in vals]
    denom = exps[0]
    for e in exps[1:]:
        denom = denom + e
    wt_out = jnp.zeros((tm, LANES), F32)
    for k in range(TOP_K):
        wt_out = jnp.where(lane == k, exps[k] / denom, wt_out)
    onehot = jnp.zeros((tm, LANES), F32)
    for sel in sels:
        onehot = jnp.where(sel, 1.0, onehot)
    r = lax.broadcasted_iota(I32, (tm, tm), 0)
    c = lax.broadcasted_iota(I32, (tm, tm), 1)
    strict = jnp.where(c < r, 1.0, 0.0).astype(BF16)
    before = carry_ref[...] + _dot(strict, onehot.astype(BF16))
    rank_out = jnp.zeros((tm, LANES), I32)
    for k in range(TOP_K):
        rk = jnp.sum(jnp.where(sels[k], before, 0.0), axis=-1, keepdims=True)
        rank_out = jnp.where(lane == k, rk.astype(I32), rank_out)
    carry_ref[...] = carry_ref[...] + jnp.sum(onehot, axis=0, keepdims=True)
    idx_ref[...] = idx_out
    wt_ref[...] = wt_out
    rank_ref[...] = rank_out
    cnt_ref[...] = carry_ref[...]


def _router(x1, g, sc, sh, wr_cat, br_pad, seq, n_exp):
    t, d = x1.shape
    tm = _tile(seq, 256)
    bpb = seq // tm
    tok = pl.BlockSpec((tm, LANES), lambda i: (i, 0))
    return pl.pallas_call(
        functools.partial(_router_kernel, n_exp=n_exp),
        grid=(t // tm,),
        in_specs=[pl.BlockSpec((tm, d), lambda i: (i, 0)),
                  pl.BlockSpec((1, d), lambda i: (0, 0)),
                  pl.BlockSpec((None, 1, d), lambda i: (i // bpb, 0, 0)),
                  pl.BlockSpec((None, 1, d), lambda i: (i // bpb, 0, 0)),
                  pl.BlockSpec((d, LANES), lambda i: (0, 0)),
                  pl.BlockSpec((1, LANES), lambda i: (0, 0))],
        out_specs=[pl.BlockSpec((tm, d // 2), lambda i: (i, 0)), tok, tok, tok,
                   pl.BlockSpec((1, LANES), lambda i: (0, 0))],
        out_shape=[jax.ShapeDtypeStruct((t, d // 2), U32),
                   jax.ShapeDtypeStruct((t, LANES), I32),
                   jax.ShapeDtypeStruct((t, LANES), F32),
                   jax.ShapeDtypeStruct((t, LANES), I32),
                   jax.ShapeDtypeStruct((1, LANES), F32)],
        scratch_shapes=[pltpu.VMEM((1, LANES), F32)],
        compiler_params=_params("arbitrary"),
        name="norm_router_topk",
    )(x1, g, sc, sh, wr_cat, br_pad)


def _dispatch_kernel(pad_end_ref, padded_ref, pos_ref, hp_ref, xs_ref, zero_ref, zsem, sem, *, tm, n_exp):
    @pl.when(pl.program_id(0) == 0)
    def _():
        zero_ref[...] = jnp.zeros_like(zero_ref)

        def zero_block(b):
            start = pl.multiple_of(b * MOE_ROWS, MOE_ROWS)
            return pltpu.make_async_copy(zero_ref, xs_ref.at[pl.ds(start, MOE_ROWS)], zsem)

        def start_tail(b, carry):
            zero_block(b).start()
            return carry

        def wait_tail(b, carry):
            zero_block(b).wait()
            return carry

        n_used = pad_end_ref[n_exp - 1] // MOE_ROWS
        n_blocks = xs_ref.shape[0] // MOE_ROWS
        for e in range(n_exp):
            @pl.when(padded_ref[e] > 0)
            def _():
                zero_block(pad_end_ref[e] // MOE_ROWS - 1).start()
        lax.fori_loop(n_used, n_blocks, start_tail, 0)
        for e in range(n_exp):
            @pl.when(padded_ref[e] > 0)
            def _():
                zero_block(pad_end_ref[e] // MOE_ROWS - 1).wait()
        lax.fori_loop(n_used, n_blocks, wait_tail, 0)

    def row_copy(r, p):
        return pltpu.make_async_copy(hp_ref.at[pl.ds(r, 1)], xs_ref.at[pl.ds(p, 1)], sem)

    def start(r, carry):
        for k in range(TOP_K):
            row_copy(r, pos_ref[r * TOP_K + k]).start(priority=k % 2)
        return carry

    def wait(r, carry):
        for k in range(TOP_K):
            row_copy(r, pos_ref[r * TOP_K + k]).wait()
        return carry

    lax.fori_loop(0, tm, start, 0)
    lax.fori_loop(0, tm, wait, 0)


def _dispatch(pad_end, padded, pos_flat, hp, n_rows):
    t, half = hp.shape
    tm = _tile(t, 256)
    n_exp = pad_end.shape[0]
    grid_spec = pltpu.PrefetchScalarGridSpec(
        num_scalar_prefetch=2,
        grid=(t // tm,),
        in_specs=[pl.BlockSpec((tm * TOP_K,), lambda i, pe, pd: (i,), memory_space=pltpu.SMEM),
                  pl.BlockSpec((tm, half), lambda i, pe, pd: (i, 0))],
        out_specs=pl.BlockSpec(memory_space=pl.ANY),
        scratch_shapes=[pltpu.VMEM((MOE_ROWS, half), U32),
                        pltpu.SemaphoreType.DMA(()), pltpu.SemaphoreType.DMA(())],
    )
    return pl.pallas_call(
        functools.partial(_dispatch_kernel, tm=tm, n_exp=n_exp),
        grid_spec=grid_spec,
        out_shape=jax.ShapeDtypeStruct((n_rows, half), U32),
        compiler_params=_params("arbitrary"),
        name="moe_dispatch",
    )(pad_end, padded, pos_flat, hp)


def _weight_runs(blk_exp, n_outer):
    n_blocks = blk_exp.shape[0]
    total = n_outer * n_blocks
    e_flat = jnp.tile(blk_exp, n_outer)
    j_flat = jnp.repeat(jnp.arange(n_outer, dtype=I32), n_blocks)
    changed = (e_flat[1:] != e_flat[:-1]) | (j_flat[1:] != j_flat[:-1])
    first = jnp.concatenate([jnp.ones((1,), bool), changed])
    steps = jnp.arange(total, dtype=I32)
    nxt = lax.cummin(jnp.where(first, steps, total)[::-1])[::-1]
    nxt = jnp.concatenate([nxt[1:], jnp.full((1,), total, I32)])
    has_next = nxt < total
    nxt_c = jnp.minimum(nxt, total - 1)
    return (first.astype(I32), has_next.astype(I32), e_flat[nxt_c].astype(I32), j_flat[nxt_c].astype(I32))


def _expert_up_kernel(be_ref, nv_ref, first_ref, pf_ok_ref, pf_e_ref, pf_j_ref,
                      x_ref, bg_ref, bu_ref, wg_hbm, wu_hbm, h_ref, stage_ref, wbf_ref, sem, *, tf):
    j, i = pl.program_id(0), pl.program_id(1)
    s = j * pl.num_programs(1) + i

    def w_copies(e, jj):
        cols = pl.ds(pl.multiple_of(jj * tf, tf), tf)
        return [pltpu.make_async_copy(w.at[0, e, :, cols], stage_ref.at[t], sem.at[t])
                for t, w in enumerate((wg_hbm, wu_hbm))]

    @pl.when(s == 0)
    def _():
        for cp in w_copies(be_ref[0], 0):
            cp.start()

    @pl.when(first_ref[s] == 1)
    def _():
        for cp in w_copies(be_ref[i], j):
            cp.wait()
        wbf_ref[0] = stage_ref[0].astype(BF16)
        wbf_ref[1] = stage_ref[1].astype(BF16)

        @pl.when(pf_ok_ref[s] == 1)
        def _():
            for cp in w_copies(pf_e_ref[s], pf_j_ref[s]):
                cp.start()

    valid = i < nv_ref[0]

    @pl.when(valid)
    def _():
        xp = x_ref[...]
        half = xp.shape[1]
        x_lo = _unpack_lo(xp).astype(BF16)
        x_hi = _unpack_hi(xp).astype(BF16)
        glu = _dot(x_lo, wbf_ref[0, :half, :]) + _dot(x_hi, wbf_ref[0, half:, :]) + bg_ref[...]
        lin = _dot(x_lo, wbf_ref[1, :half, :]) + _dot(x_hi, wbf_ref[1, half:, :]) + bu_ref[...]
        glu = jnp.minimum(glu, SWIGLU_LIMIT)
        lin = jnp.clip(lin, -SWIGLU_LIMIT, SWIGLU_LIMIT)
        h_ref[...] = (glu * _sigmoid(SWIGLU_ALPHA * glu) * (lin + 1.0)).astype(h_ref.dtype)

    @pl.when(jnp.logical_not(valid))
    def _():
        h_ref[...] = jnp.zeros_like(h_ref)


def _expert_up(blk_exp, n_valid, xs, wg, bg, wu, bu, tf):
    n_rows, half = xs.shape
    d, f = wg.shape[-2], wg.shape[-1]
    n_blocks = n_rows // MOE_ROWS
    first, pf_ok, pf_e, pf_j = _weight_runs(blk_exp, f // tf)
    last = lambda i, nv: jnp.minimum(i, nv[0] - 1)
    grid_spec = pltpu.PrefetchScalarGridSpec(
        num_scalar_prefetch=6,
        grid=(f // tf, n_blocks),
        in_specs=[pl.BlockSpec((MOE_ROWS, half), lambda j, i, be, nv, *_: (last(i, nv), 0)),
                  pl.BlockSpec((None, 1, tf), lambda j, i, be, nv, *_: (be[i], 0, j)),
                  pl.BlockSpec((None, 1, tf), lambda j, i, be, nv, *_: (be[i], 0, j)),
                  pl.BlockSpec(memory_space=pl.ANY),
                  pl.BlockSpec(memory_space=pl.ANY)],
        out_specs=pl.BlockSpec((MOE_ROWS, tf), lambda j, i, be, nv, *_: (i, j)),
        scratch_shapes=[pltpu.VMEM((2, d, tf), F32), pltpu.VMEM((2, d, tf), BF16),
                        pltpu.SemaphoreType.DMA((2,))],
    )
    return pl.pallas_call(
        functools.partial(_expert_up_kernel, tf=tf),
        grid_spec=grid_spec,
        out_shape=jax.ShapeDtypeStruct((n_rows, f), BF16),
        compiler_params=_params("arbitrary", "arbitrary"),
        name="expert_up",
    )(blk_exp, n_valid, first, pf_ok, pf_e, pf_j, xs, bg, bu, wg, wu)


def _expert_down_kernel(be_ref, nv_ref, first_ref, pf_ok_ref, pf_e_ref, pf_j_ref,
                        h_ref, bd_ref, wd_hbm, o_ref, stage_ref, wbf_ref, sem, *, tn):
    j, i = pl.program_id(0), pl.program_id(1)
    s = j * pl.num_programs(1) + i

    def w_copy(e, jj):
        cols = pl.ds(pl.multiple_of(jj * tn, tn), tn)
        return pltpu.make_async_copy(wd_hbm.at[0, e, :, cols], stage_ref, sem)

    @pl.when(s == 0)
    def _():
        w_copy(be_ref[0], 0).start()

    @pl.when(first_ref[s] == 1)
    def _():
        w_copy(be_ref[i], j).wait()
        wbf_ref[...] = stage_ref[...].astype(BF16)

        @pl.when(pf_ok_ref[s] == 1)
        def _():
            w_copy(pf_e_ref[s], pf_j_ref[s]).start()

    valid = i < nv_ref[0]

    @pl.when(valid)
    def _():
        y = _dot(h_ref[...], wbf_ref[...]) + bd_ref[...]
        o_ref[...] = _pack_halves(y)

    @pl.when(jnp.logical_not(valid))
    def _():
        o_ref[...] = jnp.zeros_like(o_ref)


def _expert_down(blk_exp, n_valid, hid, wd, bd, tn):
    n_rows, f = hid.shape
    d = wd.shape[-1]
    n_blocks = n_rows // MOE_ROWS
    first, pf_ok, pf_e, pf_j = _weight_runs(blk_exp, d // tn)
    last = lambda i, nv: jnp.minimum(i, nv[0] - 1)
    grid_spec = pltpu.PrefetchScalarGridSpec(
        num_scalar_prefetch=6,
        grid=(d // tn, n_blocks),
        in_specs=[pl.BlockSpec((MOE_ROWS, f), lambda j, i, be, nv, *_: (last(i, nv), 0)),
                  pl.BlockSpec((None, 1, tn), lambda j, i, be, nv, *_: (be[i], 0, j)),
                  pl.BlockSpec(memory_space=pl.ANY)],
        out_specs=pl.BlockSpec((MOE_ROWS, tn // 2), lambda j, i, be, nv, *_: (i, j)),
        scratch_shapes=[pltpu.VMEM((f, tn), F32), pltpu.VMEM((f, tn), BF16), pltpu.SemaphoreType.DMA(())],
    )
    return pl.pallas_call(
        functools.partial(_expert_down_kernel, tn=tn),
        grid_spec=grid_spec,
        out_shape=jax.ShapeDtypeStruct((n_rows, d // 2), U32),
        compiler_params=_params("arbitrary", "arbitrary"),
        name="expert_down",
    )(blk_exp, n_valid, first, pf_ok, pf_e, pf_j, hid, bd, wd)


def _combine_kernel(pos_ref, pos_next_ref, wt_ref, x_ref, g2_ref, gf_ref, ys_ref, o_ref, buf_ref, sem,
                    *, tm, tn):
    i = pl.program_id(0)
    last = i + 1 == pl.num_programs(0)
    d = x_ref.shape[1]
    hw = tn // 2
    group = 8

    def row_copy(sl, r, k, p):
        return pltpu.make_async_copy(ys_ref.at[pl.ds(p, 1)], buf_ref.at[sl, k, pl.ds(r, 1)], sem.at[sl])

    def slot_wait(sl):
        for k in range(TOP_K):
            pltpu.make_async_copy(ys_ref.at[pl.ds(0, tm)], buf_ref.at[sl, k], sem.at[sl]).wait()

    @pl.when(i == 0)
    def _():
        def body(r, carry):
            for k in range(TOP_K):
                row_copy(0, r, k, pos_ref[r * TOP_K + k]).start(priority=k % 2)
            return carry
        lax.fori_loop(0, tm, body, 0)

    def step(cur, nxt):
        slot_wait(cur)

        def body(g, carry):
            r0 = pl.multiple_of(g * group, group)
            for rr in range(group):
                for k in range(TOP_K):
                    row_copy(nxt, r0 + rr, k, pos_next_ref[(r0 + rr) * TOP_K + k]).start(priority=k % 2)
            rows = pl.ds(r0, group)
            wts = wt_ref[rows, :]
            pieces = []
            ssq = jnp.zeros((group, 1), F32)
            for j in range(d // tn):
                for part, unpack in ((0, _unpack_lo), (1, _unpack_hi)):
                    cols = slice(j * tn + part * hw, j * tn + (part + 1) * hw)
                    y = jnp.zeros((group, hw), F32)
                    for k in range(TOP_K):
                        y = y + wts[:, k:k + 1] * unpack(buf_ref[cur, k, rows, j * hw:(j + 1) * hw])
                    x2 = x_ref[rows, cols] + g2_ref[:, cols] * y
                    ssq = ssq + jnp.sum(x2 * x2, axis=-1, keepdims=True)
                    pieces.append((cols, x2))
            inv = lax.rsqrt(ssq / d + EPS)
            for cols, x2 in pieces:
                o_ref[rows, cols] = x2 * inv * gf_ref[:, cols]
            return carry

        lax.fori_loop(0, tm // group, body, 0)

        @pl.when(last)
        def _():
            slot_wait(nxt)

    @pl.when(i % 2 == 0)
    def _():
        step(0, 1)

    @pl.when(i % 2 == 1)
    def _():
        step(1, 0)


def _combine(pos_flat, wt, x1, g2, gf, ys, seq, tn):
    t, d = x1.shape
    tm = _tile(seq, 128)
    bpb = seq // tm
    n_steps = t // tm
    return pl.pallas_call(
        functools.partial(_combine_kernel, tm=tm, tn=tn),
        grid=(n_steps,),
        in_specs=[pl.BlockSpec((tm * TOP_K,), lambda i: (i,), memory_space=pltpu.SMEM),
                  pl.BlockSpec((tm * TOP_K,), lambda i: (jnp.minimum(i + 1, n_steps - 1),),
                               memory_space=pltpu.SMEM),
                  pl.BlockSpec((tm, LANES), lambda i: (i, 0)),
                  pl.BlockSpec((tm, d), lambda i: (i, 0)),
                  pl.BlockSpec((None, 1, d), lambda i: (i // bpb, 0, 0)),
                  pl.BlockSpec((1, d), lambda i: (0, 0)),
                  pl.BlockSpec(memory_space=pl.ANY)],
        out_specs=pl.BlockSpec((tm, d), lambda i: (i, 0)),
        out_shape=jax.ShapeDtypeStruct((t, d), F32),
        scratch_shapes=[pltpu.VMEM((2, TOP_K, tm, d // 2), U32), pltpu.SemaphoreType.DMA((2,))],
        compiler_params=_params("arbitrary"),
        name="moe_combine_final_norm",
    )(pos_flat, pos_flat, wt, x1, g2, gf, ys)


def _layer(x2d, c_pad, batch, seq, w_ada, b_ada, norm_mix_g, w_in, w_alpha_up, b_alpha, gla_norm_g, sgu_ln_g,
           sgu_ln_b, w_spatial, b_spatial, w_branch_a, w_branch_b, w_out, norm_ffn_g, w_router, b_router,
           w_exp_gate, b_exp_gate, w_exp_up, b_exp_up, w_exp_down, b_exp_down, final_g):
    t, d = x2d.shape
    rank, key = w_alpha_up.shape
    val = w_branch_a.shape[0]
    width = w_branch_b.shape[0]
    n_exp = w_router.shape[1]
    f = w_exp_gate.shape[-1]
    assert 2 * n_exp <= LANES and rank <= LANES

    mod = _ada(c_pad, w_ada, b_ada.reshape(1, -1))[:batch].reshape(batch, 6, 1, d)
    sh1, sc1, g1, sh2, sc2, g2 = (mod[:, i] for i in range(6))

    h = _norm_mod(x2d, norm_mix_g.reshape(1, d), sc1, sh1, seq)

    tm = _tile(t, 1024)
    n_qkvg = 2 * key + 2 * val
    w_in_t = w_in.T
    qkvg = _matmul_act(h, w_in_t, tm, _tile(key, 512), 0, n_qkvg, None, "proj_qkvg")
    wup = jnp.pad(w_alpha_up, ((0, LANES - rank), (0, 0))).astype(BF16)
    log_a = _alpha(h, w_in_t, n_qkvg, rank, wup, b_alpha.reshape(1, key))
    tn_w = _tile(width, 512)
    uz = _matmul_act(h, w_in_t, tm, tn_w, n_qkvg + rank, 2 * width, "gelu", "proj_uz")
    gates = _matmul_act(h, w_in_t, tm, tn_w, n_qkvg + rank + 2 * width, 2 * d, "sigmoid", "proj_gates")

    a = _gla(qkvg, log_a, gla_norm_g.reshape(1, -1), batch, seq, key, val)
    bs_full = jnp.repeat(b_spatial.T, width // w_spatial.shape[0], axis=1)
    b = _sgu(uz, sgu_ln_g.reshape(1, width), sgu_ln_b.reshape(1, width), w_spatial, bs_full, seq, width)

    tn = _tile(d, 512)
    merged = _merge(a, b, w_branch_a, w_branch_b, gates, tm, tn)
    x1 = _outproj(merged, w_out, x2d, g1, seq, _tile(seq, 1024), tn)

    w_hi = w_router.astype(BF16)
    w_lo = (w_router - w_hi.astype(F32)).astype(BF16)
    wr_cat = jnp.pad(jnp.concatenate([w_hi, w_lo], axis=1), ((0, 0), (0, LANES - 2 * n_exp)))
    br_pad = jnp.pad(b_router.reshape(1, n_exp), ((0, 0), (0, LANES - n_exp)))
    hp, top_idx, top_w, rank_in_exp, counts = _router(x1, norm_ffn_g.reshape(1, d), sc2, sh2, wr_cat, br_pad,
                                                      seq, n_exp)

    counts = counts[0, :n_exp].astype(I32)
    padded = (counts + MOE_ROWS - 1) // MOE_ROWS * MOE_ROWS
    pad_end = jnp.cumsum(padded)
    pad_start = pad_end - padded
    top_idx = top_idx[:, :TOP_K]
    pos_flat = (pad_start[top_idx] + rank_in_exp[:, :TOP_K]).reshape(-1)
    n_rows = t * TOP_K + n_exp * MOE_ROWS
    n_blocks = n_rows // MOE_ROWS
    blk_start = jnp.arange(n_blocks, dtype=I32) * MOE_ROWS
    blk_exp = jnp.minimum(jnp.sum(blk_start[:, None] >= pad_end[None, :], axis=1), n_exp - 1).astype(I32)
    n_valid = (pad_end[-1:] // MOE_ROWS).astype(I32)

    xs = _dispatch(pad_end.astype(I32), padded.astype(I32), pos_flat, hp, n_rows)
    hid = _expert_up(blk_exp, n_valid, xs, w_exp_gate, b_exp_gate.reshape(n_exp, 1, f),
                     w_exp_up, b_exp_up.reshape(n_exp, 1, f), _tile(f, 512))
    tn_down = _tile(d, 2048)
    ys = _expert_down(blk_exp, n_valid, hid, w_exp_down, b_exp_down.reshape(n_exp, 1, d), tn_down)
    return _combine(pos_flat, top_w, x1, g2, final_g, ys, seq, tn_down)


def kernel(x, c, w_ada, b_ada, norm_mix_g, w_in, w_alpha_up, b_alpha, gla_norm_g, sgu_ln_g, sgu_ln_b,
           w_spatial, b_spatial, w_branch_a, w_branch_b, w_out, norm_ffn_g, w_router, b_router,
           w_exp_gate, b_exp_gate, w_exp_up, b_exp_up, w_exp_down, b_exp_down, norm_final_g):
    batch, seq, d = x.shape
    depth = w_ada.shape[0]
    assert depth == 1, "the final rmsnorm is fused into the last layer's combine kernel"
    c_pad = jnp.pad(c, ((0, 8 - batch % 8), (0, 0))) if batch % 8 else c
    out = _layer(x.reshape(batch * seq, d), c_pad, batch, seq, w_ada[0], b_ada[0], norm_mix_g[0], w_in[0],
                 w_alpha_up[0], b_alpha[0], gla_norm_g[0], sgu_ln_g[0], sgu_ln_b[0], w_spatial[0],
                 b_spatial[0], w_branch_a[0], w_branch_b[0], w_out[0], norm_ffn_g[0], w_router[0],
                 b_router[0], w_exp_gate, b_exp_gate[0], w_exp_up, b_exp_up[0], w_exp_down, b_exp_down[0],
                 norm_final_g.reshape(1, d))
    return out.reshape(batch, seq, d)
```

```python
import functools

import jax
import jax.numpy as jnp
from jax import lax
from jax.experimental import pallas as pl
from jax.experimental.pallas import tpu as pltpu

GLA_HEADS = 4
GLA_CHUNK = 64
GLA_TAU = 16.0
SGU_CHUNK = 128
TOP_K = 4
SWIGLU_ALPHA = 1.702
SWIGLU_LIMIT = 7.0
EPS = 1e-5

LANES = 128
MOE_ROWS = 512
VMEM_LIMIT = 56 * 1024 * 1024

F32 = jnp.float32
BF16 = jnp.bfloat16
U32 = jnp.uint32
I32 = jnp.int32


def _params(*sem):
    return pltpu.CompilerParams(dimension_semantics=sem, vmem_limit_bytes=VMEM_LIMIT)


def _tile(n, want):
    if n <= want:
        return n
    t = want
    while t >= LANES:
        if n % t == 0:
            return t
        t -= LANES
    return n


def _dot(a, b):
    return jnp.dot(a, b, preferred_element_type=F32)


def _dot_nt(a, b):
    return lax.dot_general(a, b, (((1,), (1,)), ((), ())), preferred_element_type=F32)


def _dot_tn(a, b):
    return lax.dot_general(a, b, (((0,), (0,)), ((), ())), preferred_element_type=F32)


def _sigmoid(x):
    return 1.0 / (1.0 + jnp.exp(-x))


def _pack_halves(y):
    half = y.shape[1] // 2
    bits = lax.bitcast_convert_type(y.astype(BF16).astype(F32), U32)
    return (bits[:, :half] >> 16) | bits[:, half:]


def _unpack_lo(p):
    return lax.bitcast_convert_type(p << 16, F32)


def _unpack_hi(p):
    return lax.bitcast_convert_type(p & jnp.uint32(0xFFFF0000), F32)


def _ada_kernel(c_ref, w_ref, b_ref, o_ref):
    c = c_ref[...]
    s = c * _sigmoid(c)
    o_ref[...] = _dot(s.astype(BF16), w_ref[...].astype(BF16)) + b_ref[...]


def _ada(c_pad, w, b):
    rows, d = c_pad.shape
    n = w.shape[1]
    tn = _tile(n, 512)
    return pl.pallas_call(
        _ada_kernel,
        grid=(n // tn,),
        in_specs=[pl.BlockSpec((rows, d), lambda j: (0, 0)),
                  pl.BlockSpec((d, tn), lambda j: (0, j)),
                  pl.BlockSpec((1, tn), lambda j: (0, j))],
        out_specs=pl.BlockSpec((rows, tn), lambda j: (0, j)),
        out_shape=jax.ShapeDtypeStruct((rows, n), F32),
        compiler_params=_params("arbitrary"),
        name="ada_mod",
    )(c_pad, w, b)


def _norm_mod_kernel(x_ref, g_ref, sc_ref, sh_ref, o_ref):
    x = x_ref[...]
    inv = lax.rsqrt(jnp.mean(x * x, axis=-1, keepdims=True) + EPS)
    h = (x * inv) * g_ref[...] * (1.0 + sc_ref[...]) + sh_ref[...]
    o_ref[...] = h.astype(o_ref.dtype)


def _norm_mod(x2d, g, sc, sh, seq):
    t, d = x2d.shape
    tm = _tile(seq, 256)
    bpb = seq // tm
    return pl.pallas_call(
        _norm_mod_kernel,
        grid=(t // tm,),
        in_specs=[pl.BlockSpec((tm, d), lambda i: (i, 0)),
                  pl.BlockSpec((1, d), lambda i: (0, 0)),
                  pl.BlockSpec((None, 1, d), lambda i: (i // bpb, 0, 0)),
                  pl.BlockSpec((None, 1, d), lambda i: (i // bpb, 0, 0))],
        out_specs=pl.BlockSpec((tm, d), lambda i: (i, 0)),
        out_shape=jax.ShapeDtypeStruct((t, d), BF16),
        compiler_params=_params("arbitrary"),
        name="norm_mod",
    )(x2d, g, sc, sh)


def _mm_act_kernel(a_ref, wt_ref, o_ref, *, act):
    acc = _dot_nt(a_ref[...], wt_ref[...].astype(BF16))
    if act == "gelu":
        acc = jax.nn.gelu(acc)
    elif act == "sigmoid":
        acc = _sigmoid(acc)
    o_ref[...] = acc.astype(o_ref.dtype)


def _matmul_act(a, w_t, tm, tn, row0, n_rows, act, name):
    m, k = a.shape
    assert n_rows % tn == 0 and row0 % 8 == 0
    return pl.pallas_call(
        functools.partial(_mm_act_kernel, act=act),
        grid=(m // tm, n_rows // tn),
        in_specs=[pl.BlockSpec((tm, k), lambda i, j: (i, 0)),
                  pl.BlockSpec((pl.Element(tn), pl.Element(k)),
                               lambda i, j: (pl.multiple_of(row0 + j * tn, 8), 0))],
        out_specs=pl.BlockSpec((tm, tn), lambda i, j: (i, j)),
        out_shape=jax.ShapeDtypeStruct((m, n_rows), BF16),
        compiler_params=_params("arbitrary", "arbitrary"),
        name=name,
    )(a, w_t)


def _alpha_kernel(h_ref, wt_ref, wup_ref, b_ref, o_ref, *, rank):
    a = _dot_nt(h_ref[...], wt_ref[...].astype(BF16))
    lane = lax.broadcasted_iota(I32, a.shape, 1)
    a = jnp.where(lane < rank, a, 0.0)
    z = _dot(a.astype(BF16), wup_ref[...]) + b_ref[...]
    log_sig = jnp.minimum(z, 0.0) - jnp.log1p(jnp.exp(-jnp.abs(z)))
    o_ref[...] = log_sig / GLA_TAU


def _alpha(h, w_t, row0, rank, wup_pad, b_alpha):
    t, d = h.shape
    key = wup_pad.shape[1]
    tm = _tile(t, 1024)
    assert row0 % 8 == 0 and row0 + LANES <= w_t.shape[0]
    return pl.pallas_call(
        functools.partial(_alpha_kernel, rank=rank),
        grid=(t // tm,),
        in_specs=[pl.BlockSpec((tm, d), lambda i: (i, 0)),
                  pl.BlockSpec((pl.Element(LANES), pl.Element(d)), lambda i: (row0, 0)),
                  pl.BlockSpec((LANES, key), lambda i: (0, 0)),
                  pl.BlockSpec((1, key), lambda i: (0, 0))],
        out_specs=pl.BlockSpec((tm, key), lambda i: (i, 0)),
        out_shape=jax.ShapeDtypeStruct((t, key), F32),
        compiler_params=_params("arbitrary"),
        name="gla_log_alpha",
    )(h, w_t, wup_pad, b_alpha)


def _gla_kernel(q_ref, k_ref, v_ref, la_ref, g_ref, gn_ref, o_ref, st_ref, *, n_chunks, scale):
    @pl.when(pl.program_id(1) == 0)
    def _():
        st_ref[...] = jnp.zeros_like(st_ref)

    c_len = GLA_CHUNK
    dk = q_ref.shape[1] // GLA_HEADS
    dv = v_ref.shape[1] // GLA_HEADS
    row = lax.broadcasted_iota(I32, (c_len, c_len), 0)
    col = lax.broadcasted_iota(I32, (c_len, c_len), 1)
    causal = col <= row
    tri = jnp.where(causal, 1.0, 0.0).astype(BF16)
    for c in range(n_chunks):
        sl = slice(c * c_len, (c + 1) * c_len)
        for hd in range(GLA_HEADS):
            ks = slice(hd * dk, (hd + 1) * dk)
            vs = slice(hd * dv, (hd + 1) * dv)
            la = la_ref[sl, ks]
            la_hi = la.astype(BF16)
            la_lo = (la - la_hi.astype(F32)).astype(BF16)
            cum = _dot(tri, la_hi) + _dot(tri, la_lo)
            cum_last = cum[c_len - 1:c_len, :]
            q = q_ref[sl, ks].astype(F32) * scale
            k = k_ref[sl, ks].astype(F32)
            v = v_ref[sl, vs]
            q_abs = (q * jnp.exp(cum)).astype(BF16)
            q_rel = (q * jnp.exp(cum - cum_last)).astype(BF16)
            k_rel = (k * jnp.exp(cum_last - cum)).astype(BF16)
            decay = jnp.exp(cum_last)
            scores = jnp.where(causal, _dot_nt(q_rel, k_rel), 0.0).astype(BF16)
            st = st_ref[hd]
            o = _dot(scores, v) + _dot_nt(q_abs, st.astype(BF16))
            st_ref[hd] = st * decay + _dot_tn(v, k_rel)
            inv = lax.rsqrt(jnp.mean(o * o, axis=-1, keepdims=True) + EPS)
            g = g_ref[sl, vs].astype(F32)
            o_ref[sl, vs] = (o * inv * gn_ref[...] * (g * _sigmoid(g))).astype(o_ref.dtype)


def _gla(qkvg, log_a, gn, batch, seq, key, val):
    t = qkvg.shape[0]
    dk, dv = key // GLA_HEADS, val // GLA_HEADS
    blk = _tile(seq, 128)
    assert blk % GLA_CHUNK == 0 and (2 * key) % val == 0
    nb = seq // blk
    row = lambda b, n: b * nb + n
    return pl.pallas_call(
        functools.partial(_gla_kernel, n_chunks=blk // GLA_CHUNK, scale=float(dk) ** -0.5),
        grid=(batch, nb),
        in_specs=[pl.BlockSpec((blk, key), lambda b, n: (row(b, n), 0)),
                  pl.BlockSpec((blk, key), lambda b, n: (row(b, n), 1)),
                  pl.BlockSpec((blk, val), lambda b, n: (row(b, n), 2 * key // val)),
                  pl.BlockSpec((blk, key), lambda b, n: (row(b, n), 0)),
                  pl.BlockSpec((blk, val), lambda b, n: (row(b, n), 2 * key // val + 1)),
                  pl.BlockSpec((1, dv), lambda b, n: (0, 0))],
        out_specs=pl.BlockSpec((blk, val), lambda b, n: (row(b, n), 0)),
        out_shape=jax.ShapeDtypeStruct((t, val), BF16),
        scratch_shapes=[pltpu.VMEM((GLA_HEADS, dv, dk), F32)],
        compiler_params=_params("arbitrary", "arbitrary"),
        name="gla_mixer",
    )(qkvg, qkvg, qkvg, log_a, qkvg, gn)


def _sgu_kernel(u_ref, z_ref, lg_ref, lb_ref, w_ref, bs_ref, o_ref, *, n_chunks, groups):
    c_len = SGU_CHUNK
    gd = u_ref.shape[1] // groups
    row = lax.broadcasted_iota(I32, (c_len, c_len), 0)
    col = lax.broadcasted_iota(I32, (c_len, c_len), 1)
    causal = col <= row
    w_masked = [jnp.where(causal, w_ref[g], 0.0).astype(BF16) for g in range(groups)]
    for c in range(n_chunks):
        sl = slice(c * c_len, (c + 1) * c_len)
        z = z_ref[sl, :].astype(F32)
        mu = jnp.mean(z, axis=-1, keepdims=True)
        zc = z - mu
        var = jnp.mean(zc * zc, axis=-1, keepdims=True)
        zn = (zc * lax.rsqrt(var + EPS) * lg_ref[...] + lb_ref[...]).astype(BF16)
        for g in range(groups):
            gs = slice(g * gd, (g + 1) * gd)
            s = _dot(w_masked[g], zn[:, gs]) + bs_ref[:, gs]
            o_ref[sl, gs] = (u_ref[sl, gs].astype(F32) * s).astype(o_ref.dtype)


def _sgu(uz, ln_g, ln_b, w_spatial, bs_full, seq, width):
    t = uz.shape[0]
    groups = w_spatial.shape[0]
    blk = _tile(seq, 256)
    assert blk % SGU_CHUNK == 0
    return pl.pallas_call(
        functools.partial(_sgu_kernel, n_chunks=blk // SGU_CHUNK, groups=groups),
        grid=(t // blk,),
        in_specs=[pl.BlockSpec((blk, width), lambda i: (i, 0)),
                  pl.BlockSpec((blk, width), lambda i: (i, 1)),
                  pl.BlockSpec((1, width), lambda i: (0, 0)),
                  pl.BlockSpec((1, width), lambda i: (0, 0)),
                  pl.BlockSpec((groups, SGU_CHUNK, SGU_CHUNK), lambda i: (0, 0, 0)),
                  pl.BlockSpec((SGU_CHUNK, width), lambda i: (0, 0))],
        out_specs=pl.BlockSpec((blk, width), lambda i: (i, 0)),
        out_shape=jax.ShapeDtypeStruct((t, width), BF16),
        compiler_params=_params("arbitrary"),
        name="sgu_mixer",
    )(uz, uz, ln_g, ln_b, w_spatial, bs_full)


def _merge_kernel(a_ref, b_ref, wa_ref, wb_ref, ga_ref, gb_ref, o_ref):
    ya = _dot(a_ref[...], wa_ref[...].astype(BF16))
    yb = _dot(b_ref[...], wb_ref[...].astype(BF16))
    o_ref[...] = (ga_ref[...].astype(F32) * ya + gb_ref[...].astype(F32) * yb).astype(o_ref.dtype)


def _merge(a, b, wa, wb, gates, tm, tn):
    t, val = a.shape
    width = b.shape[1]
    d = wa.shape[1]
    return pl.pallas_call(
        _merge_kernel,
        grid=(t // tm, d // tn),
        in_specs=[pl.BlockSpec((tm, val), lambda i, j: (i, 0)),
                  pl.BlockSpec((tm, width), lambda i, j: (i, 0)),
                  pl.BlockSpec((val, tn), lambda i, j: (0, j)),
                  pl.BlockSpec((width, tn), lambda i, j: (0, j)),
                  pl.BlockSpec((tm, tn), lambda i, j: (i, j)),
                  pl.BlockSpec((tm, tn), lambda i, j: (i, d // tn + j))],
        out_specs=pl.BlockSpec((tm, tn), lambda i, j: (i, j)),
        out_shape=jax.ShapeDtypeStruct((t, d), BF16),
        compiler_params=_params("arbitrary", "arbitrary"),
        name="merge_branches",
    )(a, b, wa, wb, gates, gates)


def _outproj_kernel(m_ref, w_ref, x_ref, g1_ref, o_ref):
    y = _dot(m_ref[...], w_ref[...].astype(BF16))
    o_ref[...] = x_ref[...] + g1_ref[...] * y


def _outproj(merged, w, x2d, g1, seq, tm, tn):
    t, d = x2d.shape
    bpb = seq // tm
    return pl.pallas_call(
        _outproj_kernel,
        grid=(t // tm, d // tn),
        in_specs=[pl.BlockSpec((tm, d), lambda i, j: (i, 0)),
                  pl.BlockSpec((d, tn), lambda i, j: (0, j)),
                  pl.BlockSpec((tm, tn), lambda i, j: (i, j)),
                  pl.BlockSpec((None, 1, tn), lambda i, j: (i // bpb, 0, j))],
        out_specs=pl.BlockSpec((tm, tn), lambda i, j: (i, j)),
        out_shape=jax.ShapeDtypeStruct((t, d), F32),
        compiler_params=_params("arbitrary", "arbitrary"),
        name="out_proj_residual",
    )(merged, w, x2d, g1)


def _router_kernel(x_ref, g_ref, sc_ref, sh_ref, wr_ref, br_ref,
                   hp_ref, idx_ref, wt_ref, rank_ref, cnt_ref, carry_ref, *, n_exp):
    @pl.when(pl.program_id(0) == 0)
    def _():
        carry_ref[...] = jnp.zeros_like(carry_ref)

    x = x_ref[...]
    tm = x.shape[0]
    inv = lax.rsqrt(jnp.mean(x * x, axis=-1, keepdims=True) + EPS)
    h = (x * inv) * g_ref[...] * (1.0 + sc_ref[...]) + sh_ref[...]
    hp_ref[...] = _pack_halves(h)
    h_hi = h.astype(BF16)
    h_lo = (h - h_hi.astype(F32)).astype(BF16)
    p = _dot(h_hi, wr_ref[...])
    q = _dot(h_lo, wr_ref[...])
    lane = lax.broadcasted_iota(I32, (tm, LANES), 1)
    logits = p + pltpu.roll(p, LANES - n_exp, 1) + q + br_ref[...]
    neg_inf = jnp.float32(-jnp.inf)
    work = jnp.where(lane < n_exp, logits, neg_inf)
    vals, sels = [], []
    idx_out = jnp.zeros((tm, LANES), I32)
    for k in range(TOP_K):
        m = jnp.max(work, axis=-1, keepdims=True)
        idx = jnp.min(jnp.where(work == m, lane, LANES), axis=-1, keepdims=True)
        sel = lane == idx
        work = jnp.where(sel, neg_inf, work)
        vals.append(m)
        sels.append(sel)
        idx_out = jnp.where(lane == k, idx, idx_out)
    exps = [jnp.exp(v - vals[0]) for v in vals]
    denom = exps[0]
    for e in exps[1:]:
        denom = denom + e
    wt_out = jnp.zeros((tm, LANES), F32)
    for k in range(TOP_K):
        wt_out = jnp.where(lane == k, exps[k] / denom, wt_out)
    onehot = jnp.zeros((tm, LANES), F32)
    for sel in sels:
        onehot = jnp.where(sel, 1.0, onehot)
    r = lax.broadcasted_iota(I32, (tm, tm), 0)
    c = lax.broadcasted_iota(I32, (tm, tm), 1)
    strict = jnp.where(c < r, 1.0, 0.0).astype(BF16)
    before = carry_ref[...] + _dot(strict, onehot.astype(BF16))
    rank_out = jnp.zeros((tm, LANES), I32)
    for k in range(TOP_K):
        rk = jnp.sum(jnp.where(sels[k], before, 0.0), axis=-1, keepdims=True)
        rank_out = jnp.where(lane == k, rk.astype(I32), rank_out)
    carry_ref[...] = carry_ref[...] + jnp.sum(onehot, axis=0, keepdims=True)
    idx_ref[...] = idx_out
    wt_ref[...] = wt_out
    rank_ref[...] = rank_out
    cnt_ref[...] = carry_ref[...]


def _router(x1, g, sc, sh, wr_cat, br_pad, seq, n_exp):
    t, d = x1.shape
    tm = _tile(seq, 256)
    bpb = seq // tm
    tok = pl.BlockSpec((tm, LANES), lambda i: (i, 0))
    return pl.pallas_call(
        functools.partial(_router_kernel, n_exp=n_exp),
        grid=(t // tm,),
        in_specs=[pl.BlockSpec((tm, d), lambda i: (i, 0)),
                  pl.BlockSpec((1, d), lambda i: (0, 0)),
                  pl.BlockSpec((None, 1, d), lambda i: (i // bpb, 0, 0)),
                  pl.BlockSpec((None, 1, d), lambda i: (i // bpb, 0, 0)),
                  pl.BlockSpec((d, LANES), lambda i: (0, 0)),
                  pl.BlockSpec((1, LANES), lambda i: (0, 0))],
        out_specs=[pl.BlockSpec((tm, d // 2), lambda i: (i, 0)), tok, tok, tok,
                   pl.BlockSpec((1, LANES), lambda i: (0, 0))],
        out_shape=[jax.ShapeDtypeStruct((t, d // 2), U32),
                   jax.ShapeDtypeStruct((t, LANES), I32),
                   jax.ShapeDtypeStruct((t, LANES), F32),
                   jax.ShapeDtypeStruct((t, LANES), I32),
                   jax.ShapeDtypeStruct((1, LANES), F32)],
        scratch_shapes=[pltpu.VMEM((1, LANES), F32)],
        compiler_params=_params("arbitrary"),
        name="norm_router_topk",
    )(x1, g, sc, sh, wr_cat, br_pad)


def _dispatch_kernel(pad_end_ref, padded_ref, pos_ref, hp_ref, xs_ref, zero_ref, zsem, sem, *, tm, n_exp):
    @pl.when(pl.program_id(0) == 0)
    def _():
        zero_ref[...] = jnp.zeros_like(zero_ref)

        def zero_block(b):
            start = pl.multiple_of(b * MOE_ROWS, MOE_ROWS)
            return pltpu.make_async_copy(zero_ref, xs_ref.at[pl.ds(start, MOE_ROWS)], zsem)

        def start_tail(b, carry):
            zero_block(b).start()
            return carry

        def wait_tail(b, carry):
            zero_block(b).wait()
            return carry

        n_used = pad_end_ref[n_exp - 1] // MOE_ROWS
        n_blocks = xs_ref.shape[0] // MOE_ROWS
        for e in range(n_exp):
            @pl.when(padded_ref[e] > 0)
            def _():
                zero_block(pad_end_ref[e] // MOE_ROWS - 1).start()
        lax.fori_loop(n_used, n_blocks, start_tail, 0)
        for e in range(n_exp):
            @pl.when(padded_ref[e] > 0)
            def _():
                zero_block(pad_end_ref[e] // MOE_ROWS - 1).wait()
        lax.fori_loop(n_used, n_blocks, wait_tail, 0)

    def row_copy(r, p):
        return pltpu.make_async_copy(hp_ref.at[pl.ds(r, 1)], xs_ref.at[pl.ds(p, 1)], sem)

    def start(r, carry):
        for k in range(TOP_K):
            row_copy(r, pos_ref[r * TOP_K + k]).start(priority=k % 2)
        return carry

    def wait(r, carry):
        for k in range(TOP_K):
            row_copy(r, pos_ref[r * TOP_K + k]).wait()
        return carry

    lax.fori_loop(0, tm, start, 0)
    lax.fori_loop(0, tm, wait, 0)


def _dispatch(pad_end, padded, pos_flat, hp, n_rows):
    t, half = hp.shape
    tm = _tile(t, 256)
    n_exp = pad_end.shape[0]
    grid_spec = pltpu.PrefetchScalarGridSpec(
        num_scalar_prefetch=2,
        grid=(t // tm,),
        in_specs=[pl.BlockSpec((tm * TOP_K,), lambda i, pe, pd: (i,), memory_space=pltpu.SMEM),
                  pl.BlockSpec((tm, half), lambda i, pe, pd: (i, 0))],
        out_specs=pl.BlockSpec(memory_space=pl.ANY),
        scratch_shapes=[pltpu.VMEM((MOE_ROWS, half), U32),
                        pltpu.SemaphoreType.DMA(()), pltpu.SemaphoreType.DMA(())],
    )
    return pl.pallas_call(
        functools.partial(_dispatch_kernel, tm=tm, n_exp=n_exp),
        grid_spec=grid_spec,
        out_shape=jax.ShapeDtypeStruct((n_rows, half), U32),
        compiler_params=_params("arbitrary"),
        name="moe_dispatch",
    )(pad_end, padded, pos_flat, hp)


def _weight_runs(blk_exp, n_outer):
    n_blocks = blk_exp.shape[0]
    total = n_outer * n_blocks
    e_flat = jnp.tile(blk_exp, n_outer)
    j_flat = jnp.repeat(jnp.arange(n_outer, dtype=I32), n_blocks)
    changed = (e_flat[1:] != e_flat[:-1]) | (j_flat[1:] != j_flat[:-1])
    first = jnp.concatenate([jnp.ones((1,), bool), changed])
    slot = (jnp.cumsum(first.astype(I32)) - 1) % 2
    steps = jnp.arange(total, dtype=I32)
    nxt = lax.cummin(jnp.where(first, steps, total)[::-1])[::-1]
    nxt = jnp.concatenate([nxt[1:], jnp.full((1,), total, I32)])
    has_next = nxt < total
    nxt_c = jnp.minimum(nxt, total - 1)
    return (first.astype(I32), slot.astype(I32), has_next.astype(I32), e_flat[nxt_c].astype(I32),
            j_flat[nxt_c].astype(I32))


def _expert_up_kernel(be_ref, nv_ref, fill_ref, first_ref, slot_ref, pf_ok_ref, pf_e_ref, pf_j_ref,
                      x_ref, bg_ref, bu_ref, wg_hbm, wu_hbm, h_ref, stage_ref, sem, *, tf):
    del nv_ref
    j, i = pl.program_id(0), pl.program_id(1)
    s = j * pl.num_programs(1) + i
    slot = slot_ref[s]

    def w_copies(e, jj, sl):
        cols = pl.ds(pl.multiple_of(jj * tf, tf), tf)
        return [pltpu.make_async_copy(w.at[0, e, :, cols], stage_ref.at[sl, t], sem.at[sl, t])
                for t, w in enumerate((wg_hbm, wu_hbm))]

    @pl.when(s == 0)
    def _():
        for cp in w_copies(be_ref[0], 0, 0):
            cp.start()

    @pl.when(first_ref[s] == 1)
    def _():
        for cp in w_copies(be_ref[i], j, slot):
            cp.wait()

        @pl.when(pf_ok_ref[s] == 1)
        def _():
            for cp in w_copies(pf_e_ref[s], pf_j_ref[s], 1 - slot):
                cp.start()

    def compute(rows):
        xp = x_ref[:rows, :]
        half = xp.shape[1]
        x_lo = _unpack_lo(xp)
        x_hi = _unpack_hi(xp)
        wg_lo, wg_hi = stage_ref[slot, 0, :half, :], stage_ref[slot, 0, half:, :]
        wu_lo, wu_hi = stage_ref[slot, 1, :half, :], stage_ref[slot, 1, half:, :]
        glu = _dot(x_lo, wg_lo) + _dot(x_hi, wg_hi) + bg_ref[...]
        lin = _dot(x_lo, wu_lo) + _dot(x_hi, wu_hi) + bu_ref[...]
        glu = jnp.minimum(glu, SWIGLU_LIMIT)
        lin = jnp.clip(lin, -SWIGLU_LIMIT, SWIGLU_LIMIT)
        h_ref[:rows, :] = (glu * _sigmoid(SWIGLU_ALPHA * glu) * (lin + 1.0)).astype(h_ref.dtype)

    fill = fill_ref[i]
    half_rows = MOE_ROWS // 2

    @pl.when(fill > half_rows)
    def _():
        compute(MOE_ROWS)

    @pl.when(jnp.logical_and(fill > 0, fill <= half_rows))
    def _():
        compute(half_rows)
        h_ref[half_rows:, :] = jnp.zeros((MOE_ROWS - half_rows, h_ref.shape[1]), h_ref.dtype)

    @pl.when(fill == 0)
    def _():
        h_ref[...] = jnp.zeros_like(h_ref)


def _expert_up(blk_exp, n_valid, blk_fill, xs, wg, bg, wu, bu, tf):
    n_rows, half = xs.shape
    d, f = wg.shape[-2], wg.shape[-1]
    n_blocks = n_rows // MOE_ROWS
    runs = _weight_runs(blk_exp, f // tf)
    last = lambda i, nv: jnp.minimum(i, nv[0] - 1)
    grid_spec = pltpu.PrefetchScalarGridSpec(
        num_scalar_prefetch=3 + len(runs),
        grid=(f // tf, n_blocks),
        in_specs=[pl.BlockSpec((MOE_ROWS, half), lambda j, i, be, nv, *_: (last(i, nv), 0)),
                  pl.BlockSpec((None, 1, tf), lambda j, i, be, nv, *_: (be[i], 0, j)),
                  pl.BlockSpec((None, 1, tf), lambda j, i, be, nv, *_: (be[i], 0, j)),
                  pl.BlockSpec(memory_space=pl.ANY),
                  pl.BlockSpec(memory_space=pl.ANY)],
        out_specs=pl.BlockSpec((MOE_ROWS, tf), lambda j, i, be, nv, *_: (i, j)),
        scratch_shapes=[pltpu.VMEM((2, 2, d, tf), F32), pltpu.SemaphoreType.DMA((2, 2))],
    )
    return pl.pallas_call(
        functools.partial(_expert_up_kernel, tf=tf),
        grid_spec=grid_spec,
        out_shape=jax.ShapeDtypeStruct((n_rows, f), F32),
        compiler_params=_params("arbitrary", "arbitrary"),
        name="expert_up",
    )(blk_exp, n_valid, blk_fill, *runs, xs, bg, bu, wg, wu)


def _expert_down_kernel(be_ref, nv_ref, fill_ref, first_ref, slot_ref, pf_ok_ref, pf_e_ref, pf_j_ref,
                        h_ref, bd_ref, wd_hbm, o_ref, stage_ref, sem, *, tn):
    del nv_ref
    j, i = pl.program_id(0), pl.program_id(1)
    s = j * pl.num_programs(1) + i
    slot = slot_ref[s]

    def w_copy(e, jj, sl):
        cols = pl.ds(pl.multiple_of(jj * tn, tn), tn)
        return pltpu.make_async_copy(wd_hbm.at[0, e, :, cols], stage_ref.at[sl], sem.at[sl])

    @pl.when(s == 0)
    def _():
        w_copy(be_ref[0], 0, 0).start()

    @pl.when(first_ref[s] == 1)
    def _():
        w_copy(be_ref[i], j, slot).wait()

        @pl.when(pf_ok_ref[s] == 1)
        def _():
            w_copy(pf_e_ref[s], pf_j_ref[s], 1 - slot).start()

    def compute(rows):
        y = _dot(h_ref[:rows, :], stage_ref[slot]) + bd_ref[...]
        o_ref[:rows, :] = _pack_halves(y)

    fill = fill_ref[i]
    half_rows = MOE_ROWS // 2

    @pl.when(fill > half_rows)
    def _():
        compute(MOE_ROWS)

    @pl.when(jnp.logical_and(fill > 0, fill <= half_rows))
    def _():
        compute(half_rows)
        o_ref[half_rows:, :] = jnp.zeros((MOE_ROWS - half_rows, o_ref.shape[1]), o_ref.dtype)

    @pl.when(fill == 0)
    def _():
        o_ref[...] = jnp.zeros_like(o_ref)


def _expert_down(blk_exp, n_valid, blk_fill, hid, wd, bd, tn):
    n_rows, f = hid.shape
    d = wd.shape[-1]
    n_blocks = n_rows // MOE_ROWS
    runs = _weight_runs(blk_exp, d // tn)
    last = lambda i, nv: jnp.minimum(i, nv[0] - 1)
    grid_spec = pltpu.PrefetchScalarGridSpec(
        num_scalar_prefetch=3 + len(runs),
        grid=(d // tn, n_blocks),
        in_specs=[pl.BlockSpec((MOE_ROWS, f), lambda j, i, be, nv, *_: (last(i, nv), 0)),
                  pl.BlockSpec((None, 1, tn), lambda j, i, be, nv, *_: (be[i], 0, j)),
                  pl.BlockSpec(memory_space=pl.ANY)],
        out_specs=pl.BlockSpec((MOE_ROWS, tn // 2), lambda j, i, be, nv, *_: (i, j)),
        scratch_shapes=[pltpu.VMEM((2, f, tn), F32), pltpu.SemaphoreType.DMA((2,))],
    )
    return pl.pallas_call(
        functools.partial(_expert_down_kernel, tn=tn),
        grid_spec=grid_spec,
        out_shape=jax.ShapeDtypeStruct((n_rows, d // 2), U32),
        compiler_params=_params("arbitrary", "arbitrary"),
        name="expert_down",
    )(blk_exp, n_valid, blk_fill, *runs, hid, bd, wd)


def _combine_kernel(pos_ref, pos_next_ref, wt_ref, x_ref, g2_ref, gf_ref, ys_ref, o_ref, buf_ref, sem,
                    *, tm, tn):
    i = pl.program_id(0)
    slot = i % 2

    def row_copy(sl, r, k, p):
        return pltpu.make_async_copy(ys_ref.at[pl.ds(p, 1)], buf_ref.at[sl, k, pl.ds(r, 1)], sem.at[sl])

    def issue(p_ref, sl):
        def body(r, carry):
            for k in range(TOP_K):
                row_copy(sl, r, k, p_ref[r * TOP_K + k]).start(priority=k % 2)
            return carry
        lax.fori_loop(0, tm, body, 0)

    @pl.when(i == 0)
    def _():
        issue(pos_ref, 0)

    @pl.when(i + 1 < pl.num_programs(0))
    def _():
        issue(pos_next_ref, 1 - slot)

    for k in range(TOP_K):
        pltpu.make_async_copy(ys_ref.at[pl.ds(0, tm)], buf_ref.at[slot, k], sem.at[slot]).wait()

    d = x_ref.shape[1]
    hw = tn // 2
    wts = [wt_ref[:, k:k + 1] for k in range(TOP_K)]
    pieces = []
    ssq = jnp.zeros((tm, 1), F32)
    for j in range(d // tn):
        for part, unpack in ((0, _unpack_lo), (1, _unpack_hi)):
            cols = slice(j * tn + part * hw, j * tn + (part + 1) * hw)
            y = jnp.zeros((tm, hw), F32)
            for k in range(TOP_K):
                y = y + wts[k] * unpack(buf_ref[slot, k, :, j * hw:(j + 1) * hw])
            x2 = x_ref[:, cols] + g2_ref[:, cols] * y
            ssq = ssq + jnp.sum(x2 * x2, axis=-1, keepdims=True)
            pieces.append((cols, x2))
    inv = lax.rsqrt(ssq / d + EPS)
    for cols, x2 in pieces:
        o_ref[:, cols] = x2 * inv * gf_ref[:, cols]


def _combine(pos_flat, wt, x1, g2, gf, ys, seq, tn):
    t, d = x1.shape
    tm = _tile(seq, 128)
    bpb = seq // tm
    n_steps = t // tm
    return pl.pallas_call(
        functools.partial(_combine_kernel, tm=tm, tn=tn),
        grid=(n_steps,),
        in_specs=[pl.BlockSpec((tm * TOP_K,), lambda i: (i,), memory_space=pltpu.SMEM),
                  pl.BlockSpec((tm * TOP_K,), lambda i: (jnp.minimum(i + 1, n_steps - 1),),
                               memory_space=pltpu.SMEM),
                  pl.BlockSpec((tm, LANES), lambda i: (i, 0)),
                  pl.BlockSpec((tm, d), lambda i: (i, 0)),
                  pl.BlockSpec((None, 1, d), lambda i: (i // bpb, 0, 0)),
                  pl.BlockSpec((1, d), lambda i: (0, 0)),
                  pl.BlockSpec(memory_space=pl.ANY)],
        out_specs=pl.BlockSpec((tm, d), lambda i: (i, 0)),
        out_shape=jax.ShapeDtypeStruct((t, d), F32),
        scratch_shapes=[pltpu.VMEM((2, TOP_K, tm, d // 2), U32), pltpu.SemaphoreType.DMA((2,))],
        compiler_params=_params("arbitrary"),
        name="moe_combine_final_norm",
    )(pos_flat, pos_flat, wt, x1, g2, gf, ys)


def _layer(x2d, c_pad, batch, seq, w_ada, b_ada, norm_mix_g, w_in, w_alpha_up, b_alpha, gla_norm_g, sgu_ln_g,
           sgu_ln_b, w_spatial, b_spatial, w_branch_a, w_branch_b, w_out, norm_ffn_g, w_router, b_router,
           w_exp_gate, b_exp_gate, w_exp_up, b_exp_up, w_exp_down, b_exp_down, final_g):
    t, d = x2d.shape
    rank, key = w_alpha_up.shape
    val = w_branch_a.shape[0]
    width = w_branch_b.shape[0]
    n_exp = w_router.shape[1]
    f = w_exp_gate.shape[-1]
    assert 2 * n_exp <= LANES and rank <= LANES

    mod = _ada(c_pad, w_ada, b_ada.reshape(1, -1))[:batch].reshape(batch, 6, 1, d)
    sh1, sc1, g1, sh2, sc2, g2 = (mod[:, i] for i in range(6))

    h = _norm_mod(x2d, norm_mix_g.reshape(1, d), sc1, sh1, seq)

    tm = _tile(t, 1024)
    n_qkvg = 2 * key + 2 * val
    w_in_t = w_in.T
    qkvg = _matmul_act(h, w_in_t, tm, _tile(key, 512), 0, n_qkvg, None, "proj_qkvg")
    wup = jnp.pad(w_alpha_up, ((0, LANES - rank), (0, 0))).astype(BF16)
    log_a = _alpha(h, w_in_t, n_qkvg, rank, wup, b_alpha.reshape(1, key))
    tn_w = _tile(width, 512)
    uz = _matmul_act(h, w_in_t, tm, tn_w, n_qkvg + rank, 2 * width, "gelu", "proj_uz")
    gates = _matmul_act(h, w_in_t, tm, tn_w, n_qkvg + rank + 2 * width, 2 * d, "sigmoid", "proj_gates")

    a = _gla(qkvg, log_a, gla_norm_g.reshape(1, -1), batch, seq, key, val)
    bs_full = jnp.repeat(b_spatial.T, width // w_spatial.shape[0], axis=1)
    b = _sgu(uz, sgu_ln_g.reshape(1, width), sgu_ln_b.reshape(1, width), w_spatial, bs_full, seq, width)

    tn = _tile(d, 512)
    merged = _merge(a, b, w_branch_a, w_branch_b, gates, tm, tn)
    x1 = _outproj(merged, w_out, x2d, g1, seq, _tile(seq, 1024), tn)

    w_hi = w_router.astype(BF16)
    w_lo = (w_router - w_hi.astype(F32)).astype(BF16)
    wr_cat = jnp.pad(jnp.concatenate([w_hi, w_lo], axis=1), ((0, 0), (0, LANES - 2 * n_exp)))
    br_pad = jnp.pad(b_router.reshape(1, n_exp), ((0, 0), (0, LANES - n_exp)))
    hp, top_idx, top_w, rank_in_exp, counts = _router(x1, norm_ffn_g.reshape(1, d), sc2, sh2, wr_cat, br_pad,
                                                      seq, n_exp)

    counts = counts[0, :n_exp].astype(I32)
    padded = (counts + MOE_ROWS - 1) // MOE_ROWS * MOE_ROWS
    pad_end = jnp.cumsum(padded)
    pad_start = pad_end - padded
    top_idx = top_idx[:, :TOP_K]
    pos_flat = (pad_start[top_idx] + rank_in_exp[:, :TOP_K]).reshape(-1)
    n_rows = t * TOP_K + n_exp * MOE_ROWS
    n_blocks = n_rows // MOE_ROWS
    blk_start = jnp.arange(n_blocks, dtype=I32) * MOE_ROWS
    blk_exp = jnp.minimum(jnp.sum(blk_start[:, None] >= pad_end[None, :], axis=1), n_exp - 1).astype(I32)
    n_valid = (pad_end[-1:] // MOE_ROWS).astype(I32)
    blk_fill = jnp.clip((pad_start + counts)[blk_exp] - blk_start, 0, MOE_ROWS).astype(I32)

    xs = _dispatch(pad_end.astype(I32), padded.astype(I32), pos_flat, hp, n_rows)
    hid = _expert_up(blk_exp, n_valid, blk_fill, xs, w_exp_gate, b_exp_gate.reshape(n_exp, 1, f),
                     w_exp_up, b_exp_up.reshape(n_exp, 1, f), _tile(f, 512))
    tn_down = _tile(d, 4096)
    ys = _expert_down(blk_exp, n_valid, blk_fill, hid, w_exp_down, b_exp_down.reshape(n_exp, 1, d), tn_down)
    return _combine(pos_flat, top_w, x1, g2, final_g, ys, seq, tn_down)


def kernel(x, c, w_ada, b_ada, norm_mix_g, w_in, w_alpha_up, b_alpha, gla_norm_g, sgu_ln_g, sgu_ln_b,
           w_spatial, b_spatial, w_branch_a, w_branch_b, w_out, norm_ffn_g, w_router, b_router,
           w_exp_gate, b_exp_gate, w_exp_up, b_exp_up, w_exp_down, b_exp_down, norm_final_g):
    batch, seq, d = x.shape
    depth = w_ada.shape[0]
    assert depth == 1, "the final rmsnorm is fused into the last layer's combine kernel"
    c_pad = jnp.pad(c, ((0, 8 - batch % 8), (0, 0))) if batch % 8 else c
    out = _layer(x.reshape(batch * seq, d), c_pad, batch, seq, w_ada[0], b_ada[0], norm_mix_g[0], w_in[0],
                 w_alpha_up[0], b_alpha[0], gla_norm_g[0], sgu_ln_g[0], sgu_ln_b[0], w_spatial[0],
                 b_spatial[0], w_branch_a[0], w_branch_b[0], w_out[0], norm_ffn_g[0], w_router[0],
                 b_router[0], w_exp_gate, b_exp_gate[0], w_exp_up, b_exp_up[0], w_exp_down, b_exp_down[0],
                 norm_final_g.reshape(1, d))
    return out.reshape(batch, seq, d)
```

```python
import functools

import jax
import jax.numpy as jnp
from jax import lax
from jax.experimental import pallas as pl
from jax.experimental.pallas import tpu as pltpu

GLA_HEADS = 4
GLA_CHUNK = 64
GLA_TAU = 16.0
SGU_CHUNK = 128
TOP_K = 4
SWIGLU_ALPHA = 1.702
SWIGLU_LIMIT = 7.0
EPS = 1e-5

LANES = 128
MOE_ROWS = 512
VMEM_LIMIT = 56 * 1024 * 1024

F32 = jnp.float32
BF16 = jnp.bfloat16
U32 = jnp.uint32
I32 = jnp.int32


def _params(*sem):
    return pltpu.CompilerParams(dimension_semantics=sem, vmem_limit_bytes=VMEM_LIMIT)


def _tile(n, want):
    if n <= want:
        return n
    t = want
    while t >= LANES:
        if n % t == 0:
            return t
        t -= LANES
    return n


def _dot(a, b):
    return jnp.dot(a, b, preferred_element_type=F32)


def _dot_nt(a, b):
    return lax.dot_general(a, b, (((1,), (1,)), ((), ())), preferred_element_type=F32)


def _dot_tn(a, b):
    return lax.dot_general(a, b, (((0,), (0,)), ((), ())), preferred_element_type=F32)


def _sigmoid(x):
    return 1.0 / (1.0 + jnp.exp(-x))


def _pack_halves(y):
    half = y.shape[1] // 2
    bits = lax.bitcast_convert_type(y.astype(BF16).astype(F32), U32)
    return (bits[:, :half] >> 16) | bits[:, half:]


def _unpack_lo(p):
    return lax.bitcast_convert_type(p << 16, F32)


def _unpack_hi(p):
    return lax.bitcast_convert_type(p & jnp.uint32(0xFFFF0000), F32)


def _ada_kernel(c_ref, w_ref, b_ref, o_ref):
    c = c_ref[...]
    s = c * _sigmoid(c)
    o_ref[...] = _dot(s.astype(BF16), w_ref[...].astype(BF16)) + b_ref[...]


def _ada(c_pad, w, b):
    rows, d = c_pad.shape
    n = w.shape[1]
    tn = _tile(n, 512)
    return pl.pallas_call(
        _ada_kernel,
        grid=(n // tn,),
        in_specs=[pl.BlockSpec((rows, d), lambda j: (0, 0)),
                  pl.BlockSpec((d, tn), lambda j: (0, j)),
                  pl.BlockSpec((1, tn), lambda j: (0, j))],
        out_specs=pl.BlockSpec((rows, tn), lambda j: (0, j)),
        out_shape=jax.ShapeDtypeStruct((rows, n), F32),
        compiler_params=_params("arbitrary"),
        name="ada_mod",
    )(c_pad, w, b)


def _norm_mod_kernel(x_ref, g_ref, sc_ref, sh_ref, o_ref):
    x = x_ref[...]
    inv = lax.rsqrt(jnp.mean(x * x, axis=-1, keepdims=True) + EPS)
    h = (x * inv) * g_ref[...] * (1.0 + sc_ref[...]) + sh_ref[...]
    o_ref[...] = h.astype(o_ref.dtype)


def _norm_mod(x2d, g, sc, sh, seq):
    t, d = x2d.shape
    tm = _tile(seq, 256)
    bpb = seq // tm
    return pl.pallas_call(
        _norm_mod_kernel,
        grid=(t // tm,),
        in_specs=[pl.BlockSpec((tm, d), lambda i: (i, 0)),
                  pl.BlockSpec((1, d), lambda i: (0, 0)),
                  pl.BlockSpec((None, 1, d), lambda i: (i // bpb, 0, 0)),
                  pl.BlockSpec((None, 1, d), lambda i: (i // bpb, 0, 0))],
        out_specs=pl.BlockSpec((tm, d), lambda i: (i, 0)),
        out_shape=jax.ShapeDtypeStruct((t, d), BF16),
        compiler_params=_params("arbitrary"),
        name="norm_mod",
    )(x2d, g, sc, sh)


def _mm_act_kernel(a_ref, wt_ref, o_ref, *, act):
    acc = _dot_nt(a_ref[...], wt_ref[...].astype(BF16))
    if act == "gelu":
        acc = jax.nn.gelu(acc)
    elif act == "sigmoid":
        acc = _sigmoid(acc)
    o_ref[...] = acc.astype(o_ref.dtype)


def _matmul_act(a, w_t, tm, tn, row0, n_rows, act, name):
    m, k = a.shape
    assert n_rows % tn == 0 and row0 % 8 == 0
    return pl.pallas_call(
        functools.partial(_mm_act_kernel, act=act),
        grid=(m // tm, n_rows // tn),
        in_specs=[pl.BlockSpec((tm, k), lambda i, j: (i, 0)),
                  pl.BlockSpec((pl.Element(tn), pl.Element(k)),
                               lambda i, j: (pl.multiple_of(row0 + j * tn, 8), 0))],
        out_specs=pl.BlockSpec((tm, tn), lambda i, j: (i, j)),
        out_shape=jax.ShapeDtypeStruct((m, n_rows), BF16),
        compiler_params=_params("arbitrary", "arbitrary"),
        name=name,
    )(a, w_t)


def _alpha_kernel(h_ref, wt_ref, wup_ref, b_ref, o_ref, *, rank):
    a = _dot_nt(h_ref[...], wt_ref[...].astype(BF16))
    lane = lax.broadcasted_iota(I32, a.shape, 1)
    a = jnp.where(lane < rank, a, 0.0)
    z = _dot(a.astype(BF16), wup_ref[...]) + b_ref[...]
    log_sig = jnp.minimum(z, 0.0) - jnp.log1p(jnp.exp(-jnp.abs(z)))
    o_ref[...] = log_sig / GLA_TAU


def _alpha(h, w_t, row0, rank, wup_pad, b_alpha):
    t, d = h.shape
    key = wup_pad.shape[1]
    tm = _tile(t, 1024)
    assert row0 % 8 == 0 and row0 + LANES <= w_t.shape[0]
    return pl.pallas_call(
        functools.partial(_alpha_kernel, rank=rank),
        grid=(t // tm,),
        in_specs=[pl.BlockSpec((tm, d), lambda i: (i, 0)),
                  pl.BlockSpec((pl.Element(LANES), pl.Element(d)), lambda i: (row0, 0)),
                  pl.BlockSpec((LANES, key), lambda i: (0, 0)),
                  pl.BlockSpec((1, key), lambda i: (0, 0))],
        out_specs=pl.BlockSpec((tm, key), lambda i: (i, 0)),
        out_shape=jax.ShapeDtypeStruct((t, key), F32),
        compiler_params=_params("arbitrary"),
        name="gla_log_alpha",
    )(h, w_t, wup_pad, b_alpha)


def _gla_kernel(q_ref, k_ref, v_ref, la_ref, g_ref, gn_ref, o_ref, st_ref, *, n_chunks, scale):
    @pl.when(pl.program_id(1) == 0)
    def _():
        st_ref[...] = jnp.zeros_like(st_ref)

    c_len = GLA_CHUNK
    dk = q_ref.shape[1] // GLA_HEADS
    dv = v_ref.shape[1] // GLA_HEADS
    row = lax.broadcasted_iota(I32, (c_len, c_len), 0)
    col = lax.broadcasted_iota(I32, (c_len, c_len), 1)
    causal = col <= row
    tri = jnp.where(causal, 1.0, 0.0).astype(BF16)
    for c in range(n_chunks):
        sl = slice(c * c_len, (c + 1) * c_len)
        for hd in range(GLA_HEADS):
            ks = slice(hd * dk, (hd + 1) * dk)
            vs = slice(hd * dv, (hd + 1) * dv)
            la = la_ref[sl, ks]
            la_hi = la.astype(BF16)
            la_lo = (la - la_hi.astype(F32)).astype(BF16)
            cum = _dot(tri, la_hi) + _dot(tri, la_lo)
            cum_last = cum[c_len - 1:c_len, :]
            q = q_ref[sl, ks].astype(F32) * scale
            k = k_ref[sl, ks].astype(F32)
            v = v_ref[sl, vs]
            q_abs = (q * jnp.exp(cum)).astype(BF16)
            q_rel = (q * jnp.exp(cum - cum_last)).astype(BF16)
            k_rel = (k * jnp.exp(cum_last - cum)).astype(BF16)
            decay = jnp.exp(cum_last)
            scores = jnp.where(causal, _dot_nt(q_rel, k_rel), 0.0).astype(BF16)
            st = st_ref[hd]
            o = _dot(scores, v) + _dot_nt(q_abs, st.astype(BF16))
            st_ref[hd] = st * decay + _dot_tn(v, k_rel)
            inv = lax.rsqrt(jnp.mean(o * o, axis=-1, keepdims=True) + EPS)
            g = g_ref[sl, vs].astype(F32)
            o_ref[sl, vs] = (o * inv * gn_ref[...] * (g * _sigmoid(g))).astype(o_ref.dtype)


def _gla(qkvg, log_a, gn, batch, seq, key, val):
    t = qkvg.shape[0]
    dk, dv = key // GLA_HEADS, val // GLA_HEADS
    blk = _tile(seq, 256)
    assert blk % GLA_CHUNK == 0 and (2 * key) % val == 0
    nb = seq // blk
    row = lambda b, n: b * nb + n
    return pl.pallas_call(
        functools.partial(_gla_kernel, n_chunks=blk // GLA_CHUNK, scale=float(dk) ** -0.5),
        grid=(batch, nb),
        in_specs=[pl.BlockSpec((blk, key), lambda b, n: (row(b, n), 0)),
                  pl.BlockSpec((blk, key), lambda b, n: (row(b, n), 1)),
                  pl.BlockSpec((blk, val), lambda b, n: (row(b, n), 2 * key // val)),
                  pl.BlockSpec((blk, key), lambda b, n: (row(b, n), 0)),
                  pl.BlockSpec((blk, val), lambda b, n: (row(b, n), 2 * key // val + 1)),
                  pl.BlockSpec((1, dv), lambda b, n: (0, 0))],
        out_specs=pl.BlockSpec((blk, val), lambda b, n: (row(b, n), 0)),
        out_shape=jax.ShapeDtypeStruct((t, val), BF16),
        scratch_shapes=[pltpu.VMEM((GLA_HEADS, dv, dk), F32)],
        compiler_params=_params("arbitrary", "arbitrary"),
        name="gla_mixer",
    )(qkvg, qkvg, qkvg, log_a, qkvg, gn)


def _sgu_kernel(u_ref, z_ref, lg_ref, lb_ref, w_ref, bs_ref, o_ref, *, n_chunks, groups):
    c_len = SGU_CHUNK
    gd = u_ref.shape[1] // groups
    row = lax.broadcasted_iota(I32, (c_len, c_len), 0)
    col = lax.broadcasted_iota(I32, (c_len, c_len), 1)
    causal = col <= row
    w_masked = [jnp.where(causal, w_ref[g], 0.0).astype(BF16) for g in range(groups)]
    for c in range(n_chunks):
        sl = slice(c * c_len, (c + 1) * c_len)
        z = z_ref[sl, :].astype(F32)
        mu = jnp.mean(z, axis=-1, keepdims=True)
        zc = z - mu
        var = jnp.mean(zc * zc, axis=-1, keepdims=True)
        zn = (zc * lax.rsqrt(var + EPS) * lg_ref[...] + lb_ref[...]).astype(BF16)
        for g in range(groups):
            gs = slice(g * gd, (g + 1) * gd)
            s = _dot(w_masked[g], zn[:, gs]) + bs_ref[:, gs]
            o_ref[sl, gs] = (u_ref[sl, gs].astype(F32) * s).astype(o_ref.dtype)


def _sgu(uz, ln_g, ln_b, w_spatial, bs_full, seq, width):
    t = uz.shape[0]
    groups = w_spatial.shape[0]
    blk = _tile(seq, 256)
    assert blk % SGU_CHUNK == 0
    return pl.pallas_call(
        functools.partial(_sgu_kernel, n_chunks=blk // SGU_CHUNK, groups=groups),
        grid=(t // blk,),
        in_specs=[pl.BlockSpec((blk, width), lambda i: (i, 0)),
                  pl.BlockSpec((blk, width), lambda i: (i, 1)),
                  pl.BlockSpec((1, width), lambda i: (0, 0)),
                  pl.BlockSpec((1, width), lambda i: (0, 0)),
                  pl.BlockSpec((groups, SGU_CHUNK, SGU_CHUNK), lambda i: (0, 0, 0)),
                  pl.BlockSpec((SGU_CHUNK, width), lambda i: (0, 0))],
        out_specs=pl.BlockSpec((blk, width), lambda i: (i, 0)),
        out_shape=jax.ShapeDtypeStruct((t, width), BF16),
        compiler_params=_params("arbitrary"),
        name="sgu_mixer",
    )(uz, uz, ln_g, ln_b, w_spatial, bs_full)


def _merge_kernel(a_ref, b_ref, wa_ref, wb_ref, ga_ref, gb_ref, o_ref):
    ya = _dot(a_ref[...], wa_ref[...].astype(BF16))
    yb = _dot(b_ref[...], wb_ref[...].astype(BF16))
    o_ref[...] = (ga_ref[...].astype(F32) * ya + gb_ref[...].astype(F32) * yb).astype(o_ref.dtype)


def _merge(a, b, wa, wb, gates, tm, tn):
    t, val = a.shape
    width = b.shape[1]
    d = wa.shape[1]
    return pl.pallas_call(
        _merge_kernel,
        grid=(t // tm, d // tn),
        in_specs=[pl.BlockSpec((tm, val), lambda i, j: (i, 0)),
                  pl.BlockSpec((tm, width), lambda i, j: (i, 0)),
                  pl.BlockSpec((val, tn), lambda i, j: (0, j)),
                  pl.BlockSpec((width, tn), lambda i, j: (0, j)),
                  pl.BlockSpec((tm, tn), lambda i, j: (i, j)),
                  pl.BlockSpec((tm, tn), lambda i, j: (i, d // tn + j))],
        out_specs=pl.BlockSpec((tm, tn), lambda i, j: (i, j)),
        out_shape=jax.ShapeDtypeStruct((t, d), BF16),
        compiler_params=_params("arbitrary", "arbitrary"),
        name="merge_branches",
    )(a, b, wa, wb, gates, gates)


def _outproj_kernel(m_ref, w_ref, x_ref, g1_ref, o_ref):
    y = _dot(m_ref[...], w_ref[...].astype(BF16))
    o_ref[...] = x_ref[...] + g1_ref[...] * y


def _outproj(merged, w, x2d, g1, seq, tm, tn):
    t, d = x2d.shape
    bpb = seq // tm
    return pl.pallas_call(
        _outproj_kernel,
        grid=(t // tm, d // tn),
        in_specs=[pl.BlockSpec((tm, d), lambda i, j: (i, 0)),
                  pl.BlockSpec((d, tn), lambda i, j: (0, j)),
                  pl.BlockSpec((tm, tn), lambda i, j: (i, j)),
                  pl.BlockSpec((None, 1, tn), lambda i, j: (i // bpb, 0, j))],
        out_specs=pl.BlockSpec((tm, tn), lambda i, j: (i, j)),
        out_shape=jax.ShapeDtypeStruct((t, d), F32),
        compiler_params=_params("arbitrary", "arbitrary"),
        name="out_proj_residual",
    )(merged, w, x2d, g1)


def _router_kernel(x_ref, g_ref, sc_ref, sh_ref, wr_ref, br_ref,
                   hp_ref, idx_ref, wt_ref, rank_ref, cnt_ref, carry_ref, *, n_exp):
    @pl.when(pl.program_id(0) == 0)
    def _():
        carry_ref[...] = jnp.zeros_like(carry_ref)

    x = x_ref[...]
    tm = x.shape[0]
    inv = lax.rsqrt(jnp.mean(x * x, axis=-1, keepdims=True) + EPS)
    h = (x * inv) * g_ref[...] * (1.0 + sc_ref[...]) + sh_ref[...]
    hp_ref[...] = _pack_halves(h)
    h_hi = h.astype(BF16)
    h_lo = (h - h_hi.astype(F32)).astype(BF16)
    p = _dot(h_hi, wr_ref[...])
    q = _dot(h_lo, wr_ref[...])
    lane = lax.broadcasted_iota(I32, (tm, LANES), 1)
    logits = p + pltpu.roll(p, LANES - n_exp, 1) + q + br_ref[...]
    neg_inf = jnp.float32(-jnp.inf)
    work = jnp.where(lane < n_exp, logits, neg_inf)
    vals, sels = [], []
    idx_out = jnp.zeros((tm, LANES), I32)
    for k in range(TOP_K):
        m = jnp.max(work, axis=-1, keepdims=True)
        idx = jnp.min(jnp.where(work == m, lane, LANES), axis=-1, keepdims=True)
        sel = lane == idx
        work = jnp.where(sel, neg_inf, work)
        vals.append(m)
        sels.append(sel)
        idx_out = jnp.where(lane == k, idx, idx_out)
    exps = [jnp.exp(v - vals[0]) for v in vals]
    denom = exps[0]
    for e in exps[1:]:
        denom = denom + e
    wt_out = jnp.zeros((tm, LANES), F32)
    for k in range(TOP_K):
        wt_out = jnp.where(lane == k, exps[k] / denom, wt_out)
    onehot = jnp.zeros((tm, LANES), F32)
    for sel in sels:
        onehot = jnp.where(sel, 1.0, onehot)
    r = lax.broadcasted_iota(I32, (tm, tm), 0)
    c = lax.broadcasted_iota(I32, (tm, tm), 1)
    strict = jnp.where(c < r, 1.0, 0.0).astype(BF16)
    before = carry_ref[...] + _dot(strict, onehot.astype(BF16))
    rank_out = jnp.zeros((tm, LANES), I32)
    for k in range(TOP_K):
        rk = jnp.sum(jnp.where(sels[k], before, 0.0), axis=-1, keepdims=True)
        rank_out = jnp.where(lane == k, rk.astype(I32), rank_out)
    carry_ref[...] = carry_ref[...] + jnp.sum(onehot, axis=0, keepdims=True)
    idx_ref[...] = idx_out
    wt_ref[...] = wt_out
    rank_ref[...] = rank_out
    cnt_ref[...] = carry_ref[...]


def _router(x1, g, sc, sh, wr_cat, br_pad, seq, n_exp):
    t, d = x1.shape
    tm = _tile(seq, 256)
    bpb = seq // tm
    tok = pl.BlockSpec((tm, LANES), lambda i: (i, 0))
    return pl.pallas_call(
        functools.partial(_router_kernel, n_exp=n_exp),
        grid=(t // tm,),
        in_specs=[pl.BlockSpec((tm, d), lambda i: (i, 0)),
                  pl.BlockSpec((1, d), lambda i: (0, 0)),
                  pl.BlockSpec((None, 1, d), lambda i: (i // bpb, 0, 0)),
                  pl.BlockSpec((None, 1, d), lambda i: (i // bpb, 0, 0)),
                  pl.BlockSpec((d, LANES), lambda i: (0, 0)),
                  pl.BlockSpec((1, LANES), lambda i: (0, 0))],
        out_specs=[pl.BlockSpec((tm, d // 2), lambda i: (i, 0)), tok, tok, tok,
                   pl.BlockSpec((1, LANES), lambda i: (0, 0))],
        out_shape=[jax.ShapeDtypeStruct((t, d // 2), U32),
                   jax.ShapeDtypeStruct((t, LANES), I32),
                   jax.ShapeDtypeStruct((t, LANES), F32),
                   jax.ShapeDtypeStruct((t, LANES), I32),
                   jax.ShapeDtypeStruct((1, LANES), F32)],
        scratch_shapes=[pltpu.VMEM((1, LANES), F32)],
        compiler_params=_params("arbitrary"),
        name="norm_router_topk",
    )(x1, g, sc, sh, wr_cat, br_pad)


def _dispatch_kernel(pad_end_ref, padded_ref, pos_ref, hp_ref, xs_ref, zero_ref, zsem, sem, *, tm, n_exp):
    @pl.when(pl.program_id(0) == 0)
    def _():
        zero_ref[...] = jnp.zeros_like(zero_ref)

        def zero_block(b):
            start = pl.multiple_of(b * MOE_ROWS, MOE_ROWS)
            return pltpu.make_async_copy(zero_ref, xs_ref.at[pl.ds(start, MOE_ROWS)], zsem)

        def start_tail(b, carry):
            zero_block(b).start()
            return carry

        def wait_tail(b, carry):
            zero_block(b).wait()
            return carry

        n_used = pad_end_ref[n_exp - 1] // MOE_ROWS
        n_blocks = xs_ref.shape[0] // MOE_ROWS
        for e in range(n_exp):
            @pl.when(padded_ref[e] > 0)
            def _():
                zero_block(pad_end_ref[e] // MOE_ROWS - 1).start()
        lax.fori_loop(n_used, n_blocks, start_tail, 0)
        for e in range(n_exp):
            @pl.when(padded_ref[e] > 0)
            def _():
                zero_block(pad_end_ref[e] // MOE_ROWS - 1).wait()
        lax.fori_loop(n_used, n_blocks, wait_tail, 0)

    def row_copy(r, p):
        return pltpu.make_async_copy(hp_ref.at[pl.ds(r, 1)], xs_ref.at[pl.ds(p, 1)], sem)

    def start(r, carry):
        for k in range(TOP_K):
            row_copy(r, pos_ref[r * TOP_K + k]).start(priority=k % 2)
        return carry

    lax.fori_loop(0, tm, start, 0)
    for _ in range(TOP_K):
        pltpu.make_async_copy(hp_ref, xs_ref.at[pl.ds(0, tm)], sem).wait()


def _dispatch(pad_end, padded, pos_flat, hp, n_rows):
    t, half = hp.shape
    tm = _tile(t, 512)
    n_exp = pad_end.shape[0]
    grid_spec = pltpu.PrefetchScalarGridSpec(
        num_scalar_prefetch=2,
        grid=(t // tm,),
        in_specs=[pl.BlockSpec((tm * TOP_K,), lambda i, pe, pd: (i,), memory_space=pltpu.SMEM),
                  pl.BlockSpec((tm, half), lambda i, pe, pd: (i, 0))],
        out_specs=pl.BlockSpec(memory_space=pl.ANY),
        scratch_shapes=[pltpu.VMEM((MOE_ROWS, half), U32),
                        pltpu.SemaphoreType.DMA(()), pltpu.SemaphoreType.DMA(())],
    )
    return pl.pallas_call(
        functools.partial(_dispatch_kernel, tm=tm, n_exp=n_exp),
        grid_spec=grid_spec,
        out_shape=jax.ShapeDtypeStruct((n_rows, half), U32),
        compiler_params=_params("arbitrary"),
        name="moe_dispatch",
    )(pad_end, padded, pos_flat, hp)


def _weight_runs(blk_exp, n_outer):
    n_blocks = blk_exp.shape[0]
    total = n_outer * n_blocks
    e_flat = jnp.tile(blk_exp, n_outer)
    j_flat = jnp.repeat(jnp.arange(n_outer, dtype=I32), n_blocks)
    changed = (e_flat[1:] != e_flat[:-1]) | (j_flat[1:] != j_flat[:-1])
    first = jnp.concatenate([jnp.ones((1,), bool), changed])
    slot = (jnp.cumsum(first.astype(I32)) - 1) % 2
    steps = jnp.arange(total, dtype=I32)
    nxt = lax.cummin(jnp.where(first, steps, total)[::-1])[::-1]
    nxt = jnp.concatenate([nxt[1:], jnp.full((1,), total, I32)])
    has_next = nxt < total
    nxt_c = jnp.minimum(nxt, total - 1)
    return (first.astype(I32), slot.astype(I32), has_next.astype(I32), e_flat[nxt_c].astype(I32),
            j_flat[nxt_c].astype(I32))


def _for_block_fill(fill, compute, out_ref):
    quarter = MOE_ROWS // 4
    for rows in range(quarter, MOE_ROWS + 1, quarter):
        @pl.when(jnp.logical_and(fill > rows - quarter, fill <= rows))
        def _(rows=rows):
            compute(rows)
            if rows < MOE_ROWS:
                out_ref[rows:, :] = jnp.zeros((MOE_ROWS - rows, out_ref.shape[1]), out_ref.dtype)

    @pl.when(fill == 0)
    def _():
        out_ref[...] = jnp.zeros_like(out_ref)


def _expert_up_kernel(be_ref, nv_ref, fill_ref, first_ref, slot_ref, pf_ok_ref, pf_e_ref, pf_j_ref,
                      x_ref, bg_ref, bu_ref, wg_hbm, wu_hbm, h_ref, stage_ref, sem, *, tf):
    del nv_ref
    j, i = pl.program_id(0), pl.program_id(1)
    s = j * pl.num_programs(1) + i
    slot = slot_ref[s]

    def w_copies(e, jj, sl):
        cols = pl.ds(pl.multiple_of(jj * tf, tf), tf)
        return [pltpu.make_async_copy(w.at[0, e, :, cols], stage_ref.at[sl, t], sem.at[sl, t])
                for t, w in enumerate((wg_hbm, wu_hbm))]

    @pl.when(s == 0)
    def _():
        for cp in w_copies(be_ref[0], 0, 0):
            cp.start()

    @pl.when(first_ref[s] == 1)
    def _():
        for cp in w_copies(be_ref[i], j, slot):
            cp.wait()

        @pl.when(pf_ok_ref[s] == 1)
        def _():
            for cp in w_copies(pf_e_ref[s], pf_j_ref[s], 1 - slot):
                cp.start()

    def compute(rows):
        xp = x_ref[:rows, :]
        half = xp.shape[1]
        x_lo = _unpack_lo(xp)
        x_hi = _unpack_hi(xp)
        wg_lo, wg_hi = stage_ref[slot, 0, :half, :], stage_ref[slot, 0, half:, :]
        wu_lo, wu_hi = stage_ref[slot, 1, :half, :], stage_ref[slot, 1, half:, :]
        glu = _dot(x_lo, wg_lo) + _dot(x_hi, wg_hi) + bg_ref[...]
        lin = _dot(x_lo, wu_lo) + _dot(x_hi, wu_hi) + bu_ref[...]
        glu = jnp.minimum(glu, SWIGLU_LIMIT)
        lin = jnp.clip(lin, -SWIGLU_LIMIT, SWIGLU_LIMIT)
        h_ref[:rows, :] = (glu * _sigmoid(SWIGLU_ALPHA * glu) * (lin + 1.0)).astype(h_ref.dtype)

    _for_block_fill(fill_ref[i], compute, h_ref)


def _expert_up(blk_exp, n_valid, blk_fill, xs, wg, bg, wu, bu, tf):
    n_rows, half = xs.shape
    d, f = wg.shape[-2], wg.shape[-1]
    n_blocks = n_rows // MOE_ROWS
    runs = _weight_runs(blk_exp, f // tf)
    last = lambda i, nv: jnp.minimum(i, nv[0] - 1)
    grid_spec = pltpu.PrefetchScalarGridSpec(
        num_scalar_prefetch=3 + len(runs),
        grid=(f // tf, n_blocks),
        in_specs=[pl.BlockSpec((MOE_ROWS, half), lambda j, i, be, nv, *_: (last(i, nv), 0)),
                  pl.BlockSpec((None, 1, tf), lambda j, i, be, nv, *_: (be[i], 0, j)),
                  pl.BlockSpec((None, 1, tf), lambda j, i, be, nv, *_: (be[i], 0, j)),
                  pl.BlockSpec(memory_space=pl.ANY),
                  pl.BlockSpec(memory_space=pl.ANY)],
        out_specs=pl.BlockSpec((MOE_ROWS, tf), lambda j, i, be, nv, *_: (i, j)),
        scratch_shapes=[pltpu.VMEM((2, 2, d, tf), F32), pltpu.SemaphoreType.DMA((2, 2))],
    )
    return pl.pallas_call(
        functools.partial(_expert_up_kernel, tf=tf),
        grid_spec=grid_spec,
        out_shape=jax.ShapeDtypeStruct((n_rows, f), F32),
        compiler_params=_params("arbitrary", "arbitrary"),
        name="expert_up",
    )(blk_exp, n_valid, blk_fill, *runs, xs, bg, bu, wg, wu)


def _expert_down_kernel(be_ref, nv_ref, fill_ref, first_ref, slot_ref, pf_ok_ref, pf_e_ref, pf_j_ref,
                        h_ref, bd_ref, wd_hbm, o_ref, stage_ref, sem, *, tn):
    del nv_ref
    j, i = pl.program_id(0), pl.program_id(1)
    s = j * pl.num_programs(1) + i
    slot = slot_ref[s]

    def w_copy(e, jj, sl):
        cols = pl.ds(pl.multiple_of(jj * tn, tn), tn)
        return pltpu.make_async_copy(wd_hbm.at[0, e, :, cols], stage_ref.at[sl], sem.at[sl])

    @pl.when(s == 0)
    def _():
        w_copy(be_ref[0], 0, 0).start()

    @pl.when(first_ref[s] == 1)
    def _():
        w_copy(be_ref[i], j, slot).wait()

        @pl.when(pf_ok_ref[s] == 1)
        def _():
            w_copy(pf_e_ref[s], pf_j_ref[s], 1 - slot).start()

    def compute(rows):
        y = _dot(h_ref[:rows, :], stage_ref[slot]) + bd_ref[...]
        o_ref[:rows, :] = _pack_halves(y)

    _for_block_fill(fill_ref[i], compute, o_ref)


def _expert_down(blk_exp, n_valid, blk_fill, hid, wd, bd, tn):
    n_rows, f = hid.shape
    d = wd.shape[-1]
    n_blocks = n_rows // MOE_ROWS
    runs = _weight_runs(blk_exp, d // tn)
    last = lambda i, nv: jnp.minimum(i, nv[0] - 1)
    grid_spec = pltpu.PrefetchScalarGridSpec(
        num_scalar_prefetch=3 + len(runs),
        grid=(d // tn, n_blocks),
        in_specs=[pl.BlockSpec((MOE_ROWS, f), lambda j, i, be, nv, *_: (last(i, nv), 0)),
                  pl.BlockSpec((None, 1, tn), lambda j, i, be, nv, *_: (be[i], 0, j)),
                  pl.BlockSpec(memory_space=pl.ANY)],
        out_specs=pl.BlockSpec((MOE_ROWS, tn // 2), lambda j, i, be, nv, *_: (i, j)),
        scratch_shapes=[pltpu.VMEM((2, f, tn), F32), pltpu.SemaphoreType.DMA((2,))],
    )
    return pl.pallas_call(
        functools.partial(_expert_down_kernel, tn=tn),
        grid_spec=grid_spec,
        out_shape=jax.ShapeDtypeStruct((n_rows, d // 2), U32),
        compiler_params=_params("arbitrary", "arbitrary"),
        name="expert_down",
    )(blk_exp, n_valid, blk_fill, *runs, hid, bd, wd)


def _combine_kernel(pos_ref, pos_next_ref, wt_ref, x_ref, g2_ref, gf_ref, ys_ref, o_ref, buf_ref, sem,
                    *, tm, tn):
    i = pl.program_id(0)
    slot = i % 2

    def row_copy(sl, r, k, p):
        return pltpu.make_async_copy(ys_ref.at[pl.ds(p, 1)], buf_ref.at[sl, k, pl.ds(r, 1)], sem.at[sl])

    def issue(p_ref, sl):
        def body(r, carry):
            for k in range(TOP_K):
                row_copy(sl, r, k, p_ref[r * TOP_K + k]).start(priority=k % 2)
            return carry
        lax.fori_loop(0, tm, body, 0)

    @pl.when(i == 0)
    def _():
        issue(pos_ref, 0)

    @pl.when(i + 1 < pl.num_programs(0))
    def _():
        issue(pos_next_ref, 1 - slot)

    for k in range(TOP_K):
        pltpu.make_async_copy(ys_ref.at[pl.ds(0, tm)], buf_ref.at[slot, k], sem.at[slot]).wait()

    d = x_ref.shape[1]
    hw = tn // 2
    wts = [wt_ref[:, k:k + 1] for k in range(TOP_K)]
    pieces = []
    ssq = jnp.zeros((tm, 1), F32)
    for j in range(d // tn):
        for part, unpack in ((0, _unpack_lo), (1, _unpack_hi)):
            cols = slice(j * tn + part * hw, j * tn + (part + 1) * hw)
            y = jnp.zeros((tm, hw), F32)
            for k in range(TOP_K):
                y = y + wts[k] * unpack(buf_ref[slot, k, :, j * hw:(j + 1) * hw])
            x2 = x_ref[:, cols] + g2_ref[:, cols] * y
            ssq = ssq + jnp.sum(x2 * x2, axis=-1, keepdims=True)
            pieces.append((cols, x2))
    inv = lax.rsqrt(ssq / d + EPS)
    for cols, x2 in pieces:
        o_ref[:, cols] = x2 * inv * gf_ref[:, cols]


def _combine(pos_flat, wt, x1, g2, gf, ys, seq, tn):
    t, d = x1.shape
    tm = _tile(seq, 256)
    bpb = seq // tm
    n_steps = t // tm
    return pl.pallas_call(
        functools.partial(_combine_kernel, tm=tm, tn=tn),
        grid=(n_steps,),
        in_specs=[pl.BlockSpec((tm * TOP_K,), lambda i: (i,), memory_space=pltpu.SMEM),
                  pl.BlockSpec((tm * TOP_K,), lambda i: (jnp.minimum(i + 1, n_steps - 1),),
                               memory_space=pltpu.SMEM),
                  pl.BlockSpec((tm, LANES), lambda i: (i, 0)),
                  pl.BlockSpec((tm, d), lambda i: (i, 0)),
                  pl.BlockSpec((None, 1, d), lambda i: (i // bpb, 0, 0)),
                  pl.BlockSpec((1, d), lambda i: (0, 0)),
                  pl.BlockSpec(memory_space=pl.ANY)],
        out_specs=pl.BlockSpec((tm, d), lambda i: (i, 0)),
        out_shape=jax.ShapeDtypeStruct((t, d), F32),
        scratch_shapes=[pltpu.VMEM((2, TOP_K, tm, d // 2), U32), pltpu.SemaphoreType.DMA((2,))],
        compiler_params=_params("arbitrary"),
        name="moe_combine_final_norm",
    )(pos_flat, pos_flat, wt, x1, g2, gf, ys)


def _layer(x2d, c_pad, batch, seq, w_ada, b_ada, norm_mix_g, w_in, w_alpha_up, b_alpha, gla_norm_g, sgu_ln_g,
           sgu_ln_b, w_spatial, b_spatial, w_branch_a, w_branch_b, w_out, norm_ffn_g, w_router, b_router,
           w_exp_gate, b_exp_gate, w_exp_up, b_exp_up, w_exp_down, b_exp_down, final_g):
    t, d = x2d.shape
    rank, key = w_alpha_up.shape
    val = w_branch_a.shape[0]
    width = w_branch_b.shape[0]
    n_exp = w_router.shape[1]
    f = w_exp_gate.shape[-1]
    assert 2 * n_exp <= LANES and rank <= LANES

    mod = _ada(c_pad, w_ada, b_ada.reshape(1, -1))[:batch].reshape(batch, 6, 1, d)
    sh1, sc1, g1, sh2, sc2, g2 = (mod[:, i] for i in range(6))

    h = _norm_mod(x2d, norm_mix_g.reshape(1, d), sc1, sh1, seq)

    tm = _tile(t, 1024)
    n_qkvg = 2 * key + 2 * val
    w_in_t = w_in.T
    qkvg = _matmul_act(h, w_in_t, tm, _tile(key, 512), 0, n_qkvg, None, "proj_qkvg")
    wup = jnp.pad(w_alpha_up, ((0, LANES - rank), (0, 0))).astype(BF16)
    log_a = _alpha(h, w_in_t, n_qkvg, rank, wup, b_alpha.reshape(1, key))
    tn_w = _tile(width, 512)
    uz = _matmul_act(h, w_in_t, tm, tn_w, n_qkvg + rank, 2 * width, "gelu", "proj_uz")
    gates = _matmul_act(h, w_in_t, tm, tn_w, n_qkvg + rank + 2 * width, 2 * d, "sigmoid", "proj_gates")

    a = _gla(qkvg, log_a, gla_norm_g.reshape(1, -1), batch, seq, key, val)
    bs_full = jnp.repeat(b_spatial.T, width // w_spatial.shape[0], axis=1)
    b = _sgu(uz, sgu_ln_g.reshape(1, width), sgu_ln_b.reshape(1, width), w_spatial, bs_full, seq, width)

    tn = _tile(d, 512)
    merged = _merge(a, b, w_branch_a, w_branch_b, gates, tm, tn)
    x1 = _outproj(merged, w_out, x2d, g1, seq, _tile(seq, 1024), tn)

    w_hi = w_router.astype(BF16)
    w_lo = (w_router - w_hi.astype(F32)).astype(BF16)
    wr_cat = jnp.pad(jnp.concatenate([w_hi, w_lo], axis=1), ((0, 0), (0, LANES - 2 * n_exp)))
    br_pad = jnp.pad(b_router.reshape(1, n_exp), ((0, 0), (0, LANES - n_exp)))
    hp, top_idx, top_w, rank_in_exp, counts = _router(x1, norm_ffn_g.reshape(1, d), sc2, sh2, wr_cat, br_pad,
                                                      seq, n_exp)

    counts = counts[0, :n_exp].astype(I32)
    padded = (counts + MOE_ROWS - 1) // MOE_ROWS * MOE_ROWS
    pad_end = jnp.cumsum(padded)
    pad_start = pad_end - padded
    top_idx = top_idx[:, :TOP_K]
    pos_flat = (pad_start[top_idx] + rank_in_exp[:, :TOP_K]).reshape(-1)
    n_rows = t * TOP_K + n_exp * MOE_ROWS
    n_blocks = n_rows // MOE_ROWS
    blk_start = jnp.arange(n_blocks, dtype=I32) * MOE_ROWS
    blk_exp = jnp.minimum(jnp.sum(blk_start[:, None] >= pad_end[None, :], axis=1), n_exp - 1).astype(I32)
    n_valid = (pad_end[-1:] // MOE_ROWS).astype(I32)
    blk_fill = jnp.clip((pad_start + counts)[blk_exp] - blk_start, 0, MOE_ROWS).astype(I32)

    xs = _dispatch(pad_end.astype(I32), padded.astype(I32), pos_flat, hp, n_rows)
    hid = _expert_up(blk_exp, n_valid, blk_fill, xs, w_exp_gate, b_exp_gate.reshape(n_exp, 1, f),
                     w_exp_up, b_exp_up.reshape(n_exp, 1, f), _tile(f, 512))
    tn_down = _tile(d, 4096)
    ys = _expert_down(blk_exp, n_valid, blk_fill, hid, w_exp_down, b_exp_down.reshape(n_exp, 1, d), tn_down)
    return _combine(pos_flat, top_w, x1, g2, final_g, ys, seq, tn_down)


def kernel(x, c, w_ada, b_ada, norm_mix_g, w_in, w_alpha_up, b_alpha, gla_norm_g, sgu_ln_g, sgu_ln_b,
           w_spatial, b_spatial, w_branch_a, w_branch_b, w_out, norm_ffn_g, w_router, b_router,
           w_exp_gate, b_exp_gate, w_exp_up, b_exp_up, w_exp_down, b_exp_down, norm_final_g):
    batch, seq, d = x.shape
    depth = w_ada.shape[0]
    assert depth == 1, "the final rmsnorm is fused into the last layer's combine kernel"
    c_pad = jnp.pad(c, ((0, 8 - batch % 8), (0, 0))) if batch % 8 else c
    out = _layer(x.reshape(batch * seq, d), c_pad, batch, seq, w_ada[0], b_ada[0], norm_mix_g[0], w_in[0],
                 w_alpha_up[0], b_alpha[0], gla_norm_g[0], sgu_ln_g[0], sgu_ln_b[0], w_spatial[0],
                 b_spatial[0], w_branch_a[0], w_branch_b[0], w_out[0], norm_ffn_g[0], w_router[0],
                 b_router[0], w_exp_gate, b_exp_gate[0], w_exp_up, b_exp_up[0], w_exp_down, b_exp_down[0],
                 norm_final_g.reshape(1, d))
    return out.reshape(batch, seq, d)
```

```python
import functools

import jax
import jax.numpy as jnp
from jax import lax
from jax.experimental import pallas as pl
from jax.experimental.pallas import tpu as pltpu

GLA_HEADS = 4
GLA_CHUNK = 64
GLA_TAU = 16.0
SGU_CHUNK = 128
TOP_K = 4
SWIGLU_ALPHA = 1.702
SWIGLU_LIMIT = 7.0
EPS = 1e-5

LANES = 128
MOE_ROWS = 512
VMEM_LIMIT = 56 * 1024 * 1024

F32 = jnp.float32
BF16 = jnp.bfloat16
U32 = jnp.uint32
I32 = jnp.int32


def _params(*sem):
    return pltpu.CompilerParams(dimension_semantics=sem, vmem_limit_bytes=VMEM_LIMIT)


def _tile(n, want):
    if n <= want:
        return n
    t = want
    while t >= LANES:
        if n % t == 0:
            return t
        t -= LANES
    return n


def _dot(a, b):
    return jnp.dot(a, b, preferred_element_type=F32)


def _dot_nt(a, b):
    return lax.dot_general(a, b, (((1,), (1,)), ((), ())), preferred_element_type=F32)


def _dot_tn(a, b):
    return lax.dot_general(a, b, (((0,), (0,)), ((), ())), preferred_element_type=F32)


def _sigmoid(x):
    return 1.0 / (1.0 + jnp.exp(-x))


def _pack_halves(y):
    half = y.shape[1] // 2
    bits = lax.bitcast_convert_type(y.astype(BF16).astype(F32), U32)
    return (bits[:, :half] >> 16) | bits[:, half:]


def _unpack_lo(p):
    return lax.bitcast_convert_type(p << 16, F32)


def _unpack_hi(p):
    return lax.bitcast_convert_type(p & jnp.uint32(0xFFFF0000), F32)


def _ada_kernel(c_ref, w_ref, b_ref, o_ref):
    c = c_ref[...]
    s = c * _sigmoid(c)
    o_ref[...] = _dot(s.astype(BF16), w_ref[...].astype(BF16)) + b_ref[...]


def _ada(c_pad, w, b):
    rows, d = c_pad.shape
    n = w.shape[1]
    tn = _tile(n, 512)
    return pl.pallas_call(
        _ada_kernel,
        grid=(n // tn,),
        in_specs=[pl.BlockSpec((rows, d), lambda j: (0, 0)),
                  pl.BlockSpec((d, tn), lambda j: (0, j)),
                  pl.BlockSpec((1, tn), lambda j: (0, j))],
        out_specs=pl.BlockSpec((rows, tn), lambda j: (0, j)),
        out_shape=jax.ShapeDtypeStruct((rows, n), F32),
        compiler_params=_params("arbitrary"),
        name="ada_mod",
    )(c_pad, w, b)


def _norm_mod_kernel(x_ref, g_ref, sc_ref, sh_ref, o_ref):
    x = x_ref[...]
    inv = lax.rsqrt(jnp.mean(x * x, axis=-1, keepdims=True) + EPS)
    h = (x * inv) * g_ref[...] * (1.0 + sc_ref[...]) + sh_ref[...]
    o_ref[...] = h.astype(o_ref.dtype)


def _norm_mod(x2d, g, sc, sh, seq):
    t, d = x2d.shape
    tm = _tile(seq, 512)
    bpb = seq // tm
    return pl.pallas_call(
        _norm_mod_kernel,
        grid=(t // tm,),
        in_specs=[pl.BlockSpec((tm, d), lambda i: (i, 0)),
                  pl.BlockSpec((1, d), lambda i: (0, 0)),
                  pl.BlockSpec((None, 1, d), lambda i: (i // bpb, 0, 0)),
                  pl.BlockSpec((None, 1, d), lambda i: (i // bpb, 0, 0))],
        out_specs=pl.BlockSpec((tm, d), lambda i: (i, 0)),
        out_shape=jax.ShapeDtypeStruct((t, d), BF16),
        compiler_params=_params("arbitrary"),
        name="norm_mod",
    )(x2d, g, sc, sh)


def _mm_act_kernel(a_ref, wt_ref, o_ref, *, act):
    acc = _dot_nt(a_ref[...], wt_ref[...].astype(BF16))
    if act == "gelu":
        acc = jax.nn.gelu(acc)
    elif act == "sigmoid":
        acc = _sigmoid(acc)
    o_ref[...] = acc.astype(o_ref.dtype)


def _matmul_act(a, w_t, tm, tn, row0, n_rows, act, name):
    m, k = a.shape
    assert n_rows % tn == 0 and row0 % 8 == 0
    return pl.pallas_call(
        functools.partial(_mm_act_kernel, act=act),
        grid=(m // tm, n_rows // tn),
        in_specs=[pl.BlockSpec((tm, k), lambda i, j: (i, 0)),
                  pl.BlockSpec((pl.Element(tn), pl.Element(k)),
                               lambda i, j: (pl.multiple_of(row0 + j * tn, 8), 0))],
        out_specs=pl.BlockSpec((tm, tn), lambda i, j: (i, j)),
        out_shape=jax.ShapeDtypeStruct((m, n_rows), BF16),
        compiler_params=_params("arbitrary", "arbitrary"),
        name=name,
    )(a, w_t)


def _alpha_kernel(h_ref, wt_ref, wup_ref, b_ref, o_ref, *, rank):
    a = _dot_nt(h_ref[...], wt_ref[...].astype(BF16))
    lane = lax.broadcasted_iota(I32, a.shape, 1)
    a = jnp.where(lane < rank, a, 0.0)
    z = _dot(a.astype(BF16), wup_ref[...]) + b_ref[...]
    log_sig = jnp.minimum(z, 0.0) - jnp.log1p(jnp.exp(-jnp.abs(z)))
    o_ref[...] = log_sig / GLA_TAU


def _alpha(h, w_t, row0, rank, wup_pad, b_alpha):
    t, d = h.shape
    key = wup_pad.shape[1]
    tm = _tile(t, 1024)
    assert row0 % 8 == 0 and row0 + LANES <= w_t.shape[0]
    return pl.pallas_call(
        functools.partial(_alpha_kernel, rank=rank),
        grid=(t // tm,),
        in_specs=[pl.BlockSpec((tm, d), lambda i: (i, 0)),
                  pl.BlockSpec((pl.Element(LANES), pl.Element(d)), lambda i: (row0, 0)),
                  pl.BlockSpec((LANES, key), lambda i: (0, 0)),
                  pl.BlockSpec((1, key), lambda i: (0, 0))],
        out_specs=pl.BlockSpec((tm, key), lambda i: (i, 0)),
        out_shape=jax.ShapeDtypeStruct((t, key), F32),
        compiler_params=_params("arbitrary"),
        name="gla_log_alpha",
    )(h, w_t, wup_pad, b_alpha)


def _gla_kernel(q_ref, k_ref, v_ref, la_ref, g_ref, gn_ref, o_ref, st_ref, *, n_chunks, scale):
    @pl.when(pl.program_id(1) == 0)
    def _():
        st_ref[...] = jnp.zeros_like(st_ref)

    c_len = GLA_CHUNK
    blk = n_chunks * c_len
    dk = q_ref.shape[1] // GLA_HEADS
    dv = v_ref.shape[1] // GLA_HEADS
    row = lax.broadcasted_iota(I32, (blk, blk), 0)
    col = lax.broadcasted_iota(I32, (blk, blk), 1)
    causal = col <= row
    tri = jnp.where(causal, 1.0, 0.0).astype(BF16)
    row_k = lax.broadcasted_iota(I32, (blk, dk), 0)
    la = la_ref[...]
    la_hi = la.astype(BF16)
    la_lo = (la - la_hi.astype(F32)).astype(BF16)
    cum_all = _dot(tri, la_hi) + _dot(tri, la_lo)
    neg_inf = jnp.float32(-jnp.inf)
    for hd in range(GLA_HEADS):
        ks = slice(hd * dk, (hd + 1) * dk)
        vs = slice(hd * dv, (hd + 1) * dv)
        cum = cum_all[:, ks]
        ends = [cum[(c + 1) * c_len - 1:(c + 1) * c_len, :] for c in range(n_chunks)]
        own_end = jnp.concatenate([jnp.broadcast_to(e, (c_len, dk)) for e in ends], axis=0)
        q = q_ref[:, ks].astype(F32) * scale
        k = k_ref[:, ks].astype(F32)
        v = v_ref[:, vs]
        k_rel = (k * jnp.exp(own_end - cum)).astype(BF16)
        cols = []
        for c in range(n_chunks):
            expo = jnp.where(row_k >= c * c_len, cum - ends[c], neg_inf)
            q_c = (q * jnp.exp(expo)).astype(BF16)
            cols.append(_dot_nt(q_c, k_rel[c * c_len:(c + 1) * c_len, :]))
        scores = jnp.where(causal, jnp.concatenate(cols, axis=1), 0.0).astype(BF16)
        st = st_ref[hd]
        q_abs = (q * jnp.exp(cum)).astype(BF16)
        o = _dot(scores, v) + _dot_nt(q_abs, st.astype(BF16))
        k_end = (k * jnp.exp(ends[-1] - cum)).astype(BF16)
        st_ref[hd] = st * jnp.exp(ends[-1]) + _dot_tn(v, k_end)
        inv = lax.rsqrt(jnp.mean(o * o, axis=-1, keepdims=True) + EPS)
        g = g_ref[:, vs].astype(F32)
        o_ref[:, vs] = (o * inv * gn_ref[...] * (g * _sigmoid(g))).astype(o_ref.dtype)


def _gla(qkvg, log_a, gn, batch, seq, key, val):
    t = qkvg.shape[0]
    dk, dv = key // GLA_HEADS, val // GLA_HEADS
    blk = _tile(seq, 256)
    assert blk % GLA_CHUNK == 0 and (2 * key) % val == 0
    nb = seq // blk
    row = lambda b, n: b * nb + n
    return pl.pallas_call(
        functools.partial(_gla_kernel, n_chunks=blk // GLA_CHUNK, scale=float(dk) ** -0.5),
        grid=(batch, nb),
        in_specs=[pl.BlockSpec((blk, key), lambda b, n: (row(b, n), 0)),
                  pl.BlockSpec((blk, key), lambda b, n: (row(b, n), 1)),
                  pl.BlockSpec((blk, val), lambda b, n: (row(b, n), 2 * key // val)),
                  pl.BlockSpec((blk, key), lambda b, n: (row(b, n), 0)),
                  pl.BlockSpec((blk, val), lambda b, n: (row(b, n), 2 * key // val + 1)),
                  pl.BlockSpec((1, dv), lambda b, n: (0, 0))],
        out_specs=pl.BlockSpec((blk, val), lambda b, n: (row(b, n), 0)),
        out_shape=jax.ShapeDtypeStruct((t, val), BF16),
        scratch_shapes=[pltpu.VMEM((GLA_HEADS, dv, dk), F32)],
        compiler_params=_params("arbitrary", "arbitrary"),
        name="gla_mixer",
    )(qkvg, qkvg, qkvg, log_a, qkvg, gn)


def _sgu_kernel(u_ref, z_ref, lg_ref, lb_ref, w_ref, bs_ref, o_ref, *, n_chunks, groups):
    c_len = SGU_CHUNK
    gd = u_ref.shape[1] // groups
    row = lax.broadcasted_iota(I32, (c_len, c_len), 0)
    col = lax.broadcasted_iota(I32, (c_len, c_len), 1)
    causal = col <= row
    w_masked = [jnp.where(causal, w_ref[g], 0.0).astype(BF16) for g in range(groups)]
    for c in range(n_chunks):
        sl = slice(c * c_len, (c + 1) * c_len)
        z = z_ref[sl, :].astype(F32)
        mu = jnp.mean(z, axis=-1, keepdims=True)
        zc = z - mu
        var = jnp.mean(zc * zc, axis=-1, keepdims=True)
        zn = (zc * lax.rsqrt(var + EPS) * lg_ref[...] + lb_ref[...]).astype(BF16)
        for g in range(groups):
            gs = slice(g * gd, (g + 1) * gd)
            s = _dot(w_masked[g], zn[:, gs]) + bs_ref[:, gs]
            o_ref[sl, gs] = (u_ref[sl, gs].astype(F32) * s).astype(o_ref.dtype)


def _sgu(uz, ln_g, ln_b, w_spatial, bs_full, seq, width):
    t = uz.shape[0]
    groups = w_spatial.shape[0]
    blk = _tile(seq, 512)
    assert blk % SGU_CHUNK == 0
    return pl.pallas_call(
        functools.partial(_sgu_kernel, n_chunks=blk // SGU_CHUNK, groups=groups),
        grid=(t // blk,),
        in_specs=[pl.BlockSpec((blk, width), lambda i: (i, 0)),
                  pl.BlockSpec((blk, width), lambda i: (i, 1)),
                  pl.BlockSpec((1, width), lambda i: (0, 0)),
                  pl.BlockSpec((1, width), lambda i: (0, 0)),
                  pl.BlockSpec((groups, SGU_CHUNK, SGU_CHUNK), lambda i: (0, 0, 0)),
                  pl.BlockSpec((SGU_CHUNK, width), lambda i: (0, 0))],
        out_specs=pl.BlockSpec((blk, width), lambda i: (i, 0)),
        out_shape=jax.ShapeDtypeStruct((t, width), BF16),
        compiler_params=_params("arbitrary"),
        name="sgu_mixer",
    )(uz, uz, ln_g, ln_b, w_spatial, bs_full)


def _merge_kernel(a_ref, b_ref, wa_ref, wb_ref, ga_ref, gb_ref, o_ref):
    ya = _dot(a_ref[...], wa_ref[...].astype(BF16))
    yb = _dot(b_ref[...], wb_ref[...].astype(BF16))
    o_ref[...] = (ga_ref[...].astype(F32) * ya + gb_ref[...].astype(F32) * yb).astype(o_ref.dtype)


def _merge(a, b, wa, wb, gates, tm, tn):
    t, val = a.shape
    width = b.shape[1]
    d = wa.shape[1]
    return pl.pallas_call(
        _merge_kernel,
        grid=(t // tm, d // tn),
        in_specs=[pl.BlockSpec((tm, val), lambda i, j: (i, 0)),
                  pl.BlockSpec((tm, width), lambda i, j: (i, 0)),
                  pl.BlockSpec((val, tn), lambda i, j: (0, j)),
                  pl.BlockSpec((width, tn), lambda i, j: (0, j)),
                  pl.BlockSpec((tm, tn), lambda i, j: (i, j)),
                  pl.BlockSpec((tm, tn), lambda i, j: (i, d // tn + j))],
        out_specs=pl.BlockSpec((tm, tn), lambda i, j: (i, j)),
        out_shape=jax.ShapeDtypeStruct((t, d), BF16),
        compiler_params=_params("arbitrary", "arbitrary"),
        name="merge_branches",
    )(a, b, wa, wb, gates, gates)


def _outproj_kernel(m_ref, w_ref, x_ref, g1_ref, o_ref):
    y = _dot(m_ref[...], w_ref[...].astype(BF16))
    o_ref[...] = x_ref[...] + g1_ref[...] * y


def _outproj(merged, w, x2d, g1, seq, tm, tn):
    t, d = x2d.shape
    bpb = seq // tm
    return pl.pallas_call(
        _outproj_kernel,
        grid=(t // tm, d // tn),
        in_specs=[pl.BlockSpec((tm, d), lambda i, j: (i, 0)),
                  pl.BlockSpec((d, tn), lambda i, j: (0, j)),
                  pl.BlockSpec((tm, tn), lambda i, j: (i, j)),
                  pl.BlockSpec((None, 1, tn), lambda i, j: (i // bpb, 0, j))],
        out_specs=pl.BlockSpec((tm, tn), lambda i, j: (i, j)),
        out_shape=jax.ShapeDtypeStruct((t, d), F32),
        compiler_params=_params("arbitrary", "arbitrary"),
        name="out_proj_residual",
    )(merged, w, x2d, g1)


def _router_kernel(x_ref, g_ref, sc_ref, sh_ref, wr_ref, br_ref,
                   hp_ref, idx_ref, wt_ref, rank_ref, cnt_ref, carry_ref, *, n_exp):
    @pl.when(pl.program_id(0) == 0)
    def _():
        carry_ref[...] = jnp.zeros_like(carry_ref)

    x = x_ref[...]
    tm = x.shape[0]
    inv = lax.rsqrt(jnp.mean(x * x, axis=-1, keepdims=True) + EPS)
    h = (x * inv) * g_ref[...] * (1.0 + sc_ref[...]) + sh_ref[...]
    hp_ref[...] = _pack_halves(h)
    h_hi = h.astype(BF16)
    h_lo = (h - h_hi.astype(F32)).astype(BF16)
    def router_dot(a):
        pair = jnp.concatenate([a[:tm // 2], a[tm // 2:]], axis=1)
        out = _dot(pair, wr_ref[...])
        return jnp.concatenate([out[:, :LANES], out[:, LANES:]], axis=0)

    p = router_dot(h_hi)
    q = router_dot(h_lo)
    lane = lax.broadcasted_iota(I32, (tm, LANES), 1)
    logits = p + pltpu.roll(p, LANES - n_exp, 1) + q + br_ref[...]
    neg_inf = jnp.float32(-jnp.inf)
    work = jnp.where(lane < n_exp, logits, neg_inf)
    vals, sels = [], []
    idx_out = jnp.zeros((tm, LANES), I32)
    for k in range(TOP_K):
        m = jnp.max(work, axis=-1, keepdims=True)
        idx = jnp.min(jnp.where(work == m, lane, LANES), axis=-1, keepdims=True)
        sel = lane == idx
        work = jnp.where(sel, neg_inf, work)
        vals.append(m)
        sels.append(sel)
        idx_out = jnp.where(lane == k, idx, idx_out)
    exps = [jnp.exp(v - vals[0]) for v in vals]
    denom = exps[0]
    for e in exps[1:]:
        denom = denom + e
    wt_out = jnp.zeros((tm, LANES), F32)
    for k in range(TOP_K):
        wt_out = jnp.where(lane == k, exps[k] / denom, wt_out)
    onehot = jnp.zeros((tm, LANES), F32)
    for sel in sels:
        onehot = jnp.where(sel, 1.0, onehot)
    r = lax.broadcasted_iota(I32, (tm, tm), 0)
    c = lax.broadcasted_iota(I32, (tm, tm), 1)
    strict = jnp.where(c < r, 1.0, 0.0).astype(BF16)
    before = carry_ref[...] + _dot(strict, onehot.astype(BF16))
    rank_out = jnp.zeros((tm, LANES), I32)
    for k in range(TOP_K):
        rk = jnp.sum(jnp.where(sels[k], before, 0.0), axis=-1, keepdims=True)
        rank_out = jnp.where(lane == k, rk.astype(I32), rank_out)
    carry_ref[...] = carry_ref[...] + jnp.sum(onehot, axis=0, keepdims=True)
    idx_ref[...] = idx_out
    wt_ref[...] = wt_out
    rank_ref[...] = rank_out
    cnt_ref[...] = carry_ref[...]


def _router(x1, g, sc, sh, wr_cat, br_pad, seq, n_exp):
    t, d = x1.shape
    tm = _tile(seq, 256)
    bpb = seq // tm
    tok = pl.BlockSpec((tm, LANES), lambda i: (i, 0))
    return pl.pallas_call(
        functools.partial(_router_kernel, n_exp=n_exp),
        grid=(t // tm,),
        in_specs=[pl.BlockSpec((tm, d), lambda i: (i, 0)),
                  pl.BlockSpec((1, d), lambda i: (0, 0)),
                  pl.BlockSpec((None, 1, d), lambda i: (i // bpb, 0, 0)),
                  pl.BlockSpec((None, 1, d), lambda i: (i // bpb, 0, 0)),
                  pl.BlockSpec((2 * d, 2 * LANES), lambda i: (0, 0)),
                  pl.BlockSpec((1, LANES), lambda i: (0, 0))],
        out_specs=[pl.BlockSpec((tm, d // 2), lambda i: (i, 0)), tok, tok, tok,
                   pl.BlockSpec((1, LANES), lambda i: (0, 0))],
        out_shape=[jax.ShapeDtypeStruct((t, d // 2), U32),
                   jax.ShapeDtypeStruct((t, LANES), I32),
                   jax.ShapeDtypeStruct((t, LANES), F32),
                   jax.ShapeDtypeStruct((t, LANES), I32),
                   jax.ShapeDtypeStruct((1, LANES), F32)],
        scratch_shapes=[pltpu.VMEM((1, LANES), F32)],
        compiler_params=_params("arbitrary"),
        name="norm_router_topk",
    )(x1, g, sc, sh, wr_cat, br_pad)


def _dispatch_kernel(pad_end_ref, padded_ref, pos_ref, hp_ref, xs_ref, zero_ref, zsem, sem, *, tm, n_exp):
    @pl.when(pl.program_id(0) == 0)
    def _():
        zero_ref[...] = jnp.zeros_like(zero_ref)

        def zero_block(b):
            start = pl.multiple_of(b * MOE_ROWS, MOE_ROWS)
            return pltpu.make_async_copy(zero_ref, xs_ref.at[pl.ds(start, MOE_ROWS)], zsem)

        def start_tail(b, carry):
            zero_block(b).start()
            return carry

        def wait_tail(b, carry):
            zero_block(b).wait()
            return carry

        n_used = pad_end_ref[n_exp - 1] // MOE_ROWS
        n_blocks = xs_ref.shape[0] // MOE_ROWS
        for e in range(n_exp):
            @pl.when(padded_ref[e] > 0)
            def _():
                zero_block(pad_end_ref[e] // MOE_ROWS - 1).start()
        lax.fori_loop(n_used, n_blocks, start_tail, 0)
        for e in range(n_exp):
            @pl.when(padded_ref[e] > 0)
            def _():
                zero_block(pad_end_ref[e] // MOE_ROWS - 1).wait()
        lax.fori_loop(n_used, n_blocks, wait_tail, 0)

    def row_copy(r, p):
        return pltpu.make_async_copy(hp_ref.at[pl.ds(r, 1)], xs_ref.at[pl.ds(p, 1)], sem)

    def start(r, carry):
        for k in range(TOP_K):
            row_copy(r, pos_ref[r * TOP_K + k]).start(priority=k % 2)
        return carry

    lax.fori_loop(0, tm, start, 0)
    for _ in range(TOP_K):
        pltpu.make_async_copy(hp_ref, xs_ref.at[pl.ds(0, tm)], sem).wait()


def _dispatch(pad_end, padded, pos_flat, hp, n_rows):
    t, half = hp.shape
    tm = _tile(t, 512)
    n_exp = pad_end.shape[0]
    grid_spec = pltpu.PrefetchScalarGridSpec(
        num_scalar_prefetch=2,
        grid=(t // tm,),
        in_specs=[pl.BlockSpec((tm * TOP_K,), lambda i, pe, pd: (i,), memory_space=pltpu.SMEM),
                  pl.BlockSpec((tm, half), lambda i, pe, pd: (i, 0))],
        out_specs=pl.BlockSpec(memory_space=pl.ANY),
        scratch_shapes=[pltpu.VMEM((MOE_ROWS, half), U32),
                        pltpu.SemaphoreType.DMA(()), pltpu.SemaphoreType.DMA(())],
    )
    return pl.pallas_call(
        functools.partial(_dispatch_kernel, tm=tm, n_exp=n_exp),
        grid_spec=grid_spec,
        out_shape=jax.ShapeDtypeStruct((n_rows, half), U32),
        compiler_params=_params("arbitrary"),
        name="moe_dispatch",
    )(pad_end, padded, pos_flat, hp)


def _weight_runs(blk_exp, n_outer):
    n_blocks = blk_exp.shape[0]
    total = n_outer * n_blocks
    e_flat = jnp.tile(blk_exp, n_outer)
    j_flat = jnp.repeat(jnp.arange(n_outer, dtype=I32), n_blocks)
    changed = (e_flat[1:] != e_flat[:-1]) | (j_flat[1:] != j_flat[:-1])
    first = jnp.concatenate([jnp.ones((1,), bool), changed])
    slot = (jnp.cumsum(first.astype(I32)) - 1) % 2
    steps = jnp.arange(total, dtype=I32)
    nxt = lax.cummin(jnp.where(first, steps, total)[::-1])[::-1]
    nxt = jnp.concatenate([nxt[1:], jnp.full((1,), total, I32)])
    has_next = nxt < total
    nxt_c = jnp.minimum(nxt, total - 1)
    return (first.astype(I32), slot.astype(I32), has_next.astype(I32), e_flat[nxt_c].astype(I32),
            j_flat[nxt_c].astype(I32))


def _for_block_fill(fill, compute, out_ref):
    quarter = MOE_ROWS // 4
    for rows in range(quarter, MOE_ROWS + 1, quarter):
        @pl.when(jnp.logical_and(fill > rows - quarter, fill <= rows))
        def _(rows=rows):
            compute(rows)
            if rows < MOE_ROWS:
                out_ref[rows:, :] = jnp.zeros((MOE_ROWS - rows, out_ref.shape[1]), out_ref.dtype)

    @pl.when(fill == 0)
    def _():
        out_ref[...] = jnp.zeros_like(out_ref)


def _expert_up_kernel(be_ref, nv_ref, fill_ref, first_ref, slot_ref, pf_ok_ref, pf_e_ref, pf_j_ref,
                      x_ref, bg_ref, bu_ref, wg_hbm, wu_hbm, h_ref, stage_ref, sem, *, tf):
    del nv_ref
    j, i = pl.program_id(0), pl.program_id(1)
    s = j * pl.num_programs(1) + i
    slot = slot_ref[s]

    def w_copies(e, jj, sl):
        cols = pl.ds(pl.multiple_of(jj * tf, tf), tf)
        return [pltpu.make_async_copy(w.at[0, e, :, cols], stage_ref.at[sl, t], sem.at[sl, t])
                for t, w in enumerate((wg_hbm, wu_hbm))]

    @pl.when(s == 0)
    def _():
        for cp in w_copies(be_ref[0], 0, 0):
            cp.start()

    @pl.when(first_ref[s] == 1)
    def _():
        for cp in w_copies(be_ref[i], j, slot):
            cp.wait()

        @pl.when(pf_ok_ref[s] == 1)
        def _():
            for cp in w_copies(pf_e_ref[s], pf_j_ref[s], 1 - slot):
                cp.start()

    def compute(rows):
        xp = x_ref[:rows, :]
        half = xp.shape[1]
        x_lo = _unpack_lo(xp)
        x_hi = _unpack_hi(xp)
        wg_lo, wg_hi = stage_ref[slot, 0, :half, :], stage_ref[slot, 0, half:, :]
        wu_lo, wu_hi = stage_ref[slot, 1, :half, :], stage_ref[slot, 1, half:, :]
        glu = _dot(x_lo, wg_lo) + _dot(x_hi, wg_hi) + bg_ref[...]
        lin = _dot(x_lo, wu_lo) + _dot(x_hi, wu_hi) + bu_ref[...]
        glu = jnp.minimum(glu, SWIGLU_LIMIT)
        lin = jnp.clip(lin, -SWIGLU_LIMIT, SWIGLU_LIMIT)
        h_ref[:rows, :] = (glu * _sigmoid(SWIGLU_ALPHA * glu) * (lin + 1.0)).astype(h_ref.dtype)

    _for_block_fill(fill_ref[i], compute, h_ref)


def _expert_up(blk_exp, n_valid, blk_fill, xs, wg, bg, wu, bu, tf):
    n_rows, half = xs.shape
    d, f = wg.shape[-2], wg.shape[-1]
    n_blocks = n_rows // MOE_ROWS
    runs = _weight_runs(blk_exp, f // tf)
    last = lambda i, nv: jnp.minimum(i, nv[0] - 1)
    grid_spec = pltpu.PrefetchScalarGridSpec(
        num_scalar_prefetch=3 + len(runs),
        grid=(f // tf, n_blocks),
        in_specs=[pl.BlockSpec((MOE_ROWS, half), lambda j, i, be, nv, *_: (last(i, nv), 0)),
                  pl.BlockSpec((None, 1, tf), lambda j, i, be, nv, *_: (be[i], 0, j)),
                  pl.BlockSpec((None, 1, tf), lambda j, i, be, nv, *_: (be[i], 0, j)),
                  pl.BlockSpec(memory_space=pl.ANY),
                  pl.BlockSpec(memory_space=pl.ANY)],
        out_specs=pl.BlockSpec((MOE_ROWS, tf), lambda j, i, be, nv, *_: (i, j)),
        scratch_shapes=[pltpu.VMEM((2, 2, d, tf), F32), pltpu.SemaphoreType.DMA((2, 2))],
    )
    return pl.pallas_call(
        functools.partial(_expert_up_kernel, tf=tf),
        grid_spec=grid_spec,
        out_shape=jax.ShapeDtypeStruct((n_rows, f), F32),
        compiler_params=_params("arbitrary", "arbitrary"),
        name="expert_up",
    )(blk_exp, n_valid, blk_fill, *runs, xs, bg, bu, wg, wu)


def _expert_down_kernel(be_ref, nv_ref, fill_ref, first_ref, slot_ref, pf_ok_ref, pf_e_ref, pf_j_ref,
                        h_ref, bd_ref, wd_hbm, o_ref, stage_ref, sem, *, tn):
    del nv_ref
    j, i = pl.program_id(0), pl.program_id(1)
    s = j * pl.num_programs(1) + i
    slot = slot_ref[s]

    def w_copy(e, jj, sl):
        cols = pl.ds(pl.multiple_of(jj * tn, tn), tn)
        return pltpu.make_async_copy(wd_hbm.at[0, e, :, cols], stage_ref.at[sl], sem.at[sl])

    @pl.when(s == 0)
    def _():
        w_copy(be_ref[0], 0, 0).start()

    @pl.when(first_ref[s] == 1)
    def _():
        w_copy(be_ref[i], j, slot).wait()

        @pl.when(pf_ok_ref[s] == 1)
        def _():
            w_copy(pf_e_ref[s], pf_j_ref[s], 1 - slot).start()

    def compute(rows):
        y = _dot(h_ref[:rows, :], stage_ref[slot]) + bd_ref[...]
        o_ref[:rows, :] = _pack_halves(y)

    _for_block_fill(fill_ref[i], compute, o_ref)


def _expert_down(blk_exp, n_valid, blk_fill, hid, wd, bd, tn):
    n_rows, f = hid.shape
    d = wd.shape[-1]
    n_blocks = n_rows // MOE_ROWS
    runs = _weight_runs(blk_exp, d // tn)
    last = lambda i, nv: jnp.minimum(i, nv[0] - 1)
    grid_spec = pltpu.PrefetchScalarGridSpec(
        num_scalar_prefetch=3 + len(runs),
        grid=(d // tn, n_blocks),
        in_specs=[pl.BlockSpec((MOE_ROWS, f), lambda j, i, be, nv, *_: (last(i, nv), 0)),
                  pl.BlockSpec((None, 1, tn), lambda j, i, be, nv, *_: (be[i], 0, j)),
                  pl.BlockSpec(memory_space=pl.ANY)],
        out_specs=pl.BlockSpec((MOE_ROWS, tn // 2), lambda j, i, be, nv, *_: (i, j)),
        scratch_shapes=[pltpu.VMEM((2, f, tn), F32), pltpu.SemaphoreType.DMA((2,))],
    )
    return pl.pallas_call(
        functools.partial(_expert_down_kernel, tn=tn),
        grid_spec=grid_spec,
        out_shape=jax.ShapeDtypeStruct((n_rows, d // 2), U32),
        compiler_params=_params("arbitrary", "arbitrary"),
        name="expert_down",
    )(blk_exp, n_valid, blk_fill, *runs, hid, bd, wd)


def _combine_kernel(pos_ref, pos_next_ref, wt_ref, x_ref, g2_ref, gf_ref, ys_ref, o_ref, buf_ref, sem,
                    *, tm, tn):
    i = pl.program_id(0)
    slot = i % 2

    def row_copy(sl, r, k, p):
        return pltpu.make_async_copy(ys_ref.at[pl.ds(p, 1)], buf_ref.at[sl, k, pl.ds(r, 1)], sem.at[sl])

    def issue(p_ref, sl):
        def body(r, carry):
            for k in range(TOP_K):
                row_copy(sl, r, k, p_ref[r * TOP_K + k]).start(priority=k % 2)
            return carry
        lax.fori_loop(0, tm, body, 0)

    @pl.when(i == 0)
    def _():
        issue(pos_ref, 0)

    @pl.when(i + 1 < pl.num_programs(0))
    def _():
        issue(pos_next_ref, 1 - slot)

    for k in range(TOP_K):
        pltpu.make_async_copy(ys_ref.at[pl.ds(0, tm)], buf_ref.at[slot, k], sem.at[slot]).wait()

    d = x_ref.shape[1]
    hw = tn // 2
    wts = [wt_ref[:, k:k + 1] for k in range(TOP_K)]
    pieces = []
    ssq = jnp.zeros((tm, 1), F32)
    for j in range(d // tn):
        for part, unpack in ((0, _unpack_lo), (1, _unpack_hi)):
            cols = slice(j * tn + part * hw, j * tn + (part + 1) * hw)
            y = jnp.zeros((tm, hw), F32)
            for k in range(TOP_K):
                y = y + wts[k] * unpack(buf_ref[slot, k, :, j * hw:(j + 1) * hw])
            x2 = x_ref[:, cols] + g2_ref[:, cols] * y
            ssq = ssq + jnp.sum(x2 * x2, axis=-1, keepdims=True)
            pieces.append((cols, x2))
    inv = lax.rsqrt(ssq / d + EPS)
    for cols, x2 in pieces:
        o_ref[:, cols] = x2 * inv * gf_ref[:, cols]


def _combine(pos_flat, wt, x1, g2, gf, ys, seq, tn):
    t, d = x1.shape
    tm = _tile(seq, 256)
    bpb = seq // tm
    n_steps = t // tm
    return pl.pallas_call(
        functools.partial(_combine_kernel, tm=tm, tn=tn),
        grid=(n_steps,),
        in_specs=[pl.BlockSpec((tm * TOP_K,), lambda i: (i,), memory_space=pltpu.SMEM),
                  pl.BlockSpec((tm * TOP_K,), lambda i: (jnp.minimum(i + 1, n_steps - 1),),
                               memory_space=pltpu.SMEM),
                  pl.BlockSpec((tm, LANES), lambda i: (i, 0)),
                  pl.BlockSpec((tm, d), lambda i: (i, 0)),
                  pl.BlockSpec((None, 1, d), lambda i: (i // bpb, 0, 0)),
                  pl.BlockSpec((1, d), lambda i: (0, 0)),
                  pl.BlockSpec(memory_space=pl.ANY)],
        out_specs=pl.BlockSpec((tm, d), lambda i: (i, 0)),
        out_shape=jax.ShapeDtypeStruct((t, d), F32),
        scratch_shapes=[pltpu.VMEM((2, TOP_K, tm, d // 2), U32), pltpu.SemaphoreType.DMA((2,))],
        compiler_params=_params("arbitrary"),
        name="moe_combine_final_norm",
    )(pos_flat, pos_flat, wt, x1, g2, gf, ys)


def _layer(x2d, c_pad, batch, seq, w_ada, b_ada, norm_mix_g, w_in, w_alpha_up, b_alpha, gla_norm_g, sgu_ln_g,
           sgu_ln_b, w_spatial, b_spatial, w_branch_a, w_branch_b, w_out, norm_ffn_g, w_router, b_router,
           w_exp_gate, b_exp_gate, w_exp_up, b_exp_up, w_exp_down, b_exp_down, final_g):
    t, d = x2d.shape
    rank, key = w_alpha_up.shape
    val = w_branch_a.shape[0]
    width = w_branch_b.shape[0]
    n_exp = w_router.shape[1]
    f = w_exp_gate.shape[-1]
    assert 2 * n_exp <= LANES and rank <= LANES

    mod = _ada(c_pad, w_ada, b_ada.reshape(1, -1))[:batch].reshape(batch, 6, 1, d)
    sh1, sc1, g1, sh2, sc2, g2 = (mod[:, i] for i in range(6))

    h = _norm_mod(x2d, norm_mix_g.reshape(1, d), sc1, sh1, seq)

    tm = _tile(t, 1024)
    tn_dense = 512
    n_qkvg = 2 * key + 2 * val
    w_in_t = w_in.T
    qkvg = _matmul_act(h, w_in_t, tm, tn_dense, 0, n_qkvg, None, "proj_qkvg")
    wup = jnp.pad(w_alpha_up, ((0, LANES - rank), (0, 0))).astype(BF16)
    log_a = _alpha(h, w_in_t, n_qkvg, rank, wup, b_alpha.reshape(1, key))
    uz = _matmul_act(h, w_in_t, tm, tn_dense, n_qkvg + rank, 2 * width, "gelu", "proj_uz")
    gates = _matmul_act(h, w_in_t, tm, tn_dense, n_qkvg + rank + 2 * width, 2 * d, "sigmoid", "proj_gates")

    a = _gla(qkvg, log_a, gla_norm_g.reshape(1, -1), batch, seq, key, val)
    bs_full = jnp.repeat(b_spatial.T, width // w_spatial.shape[0], axis=1)
    b = _sgu(uz, sgu_ln_g.reshape(1, width), sgu_ln_b.reshape(1, width), w_spatial, bs_full, seq, width)

    merged = _merge(a, b, w_branch_a, w_branch_b, gates, _tile(t, 2048), 256)
    x1 = _outproj(merged, w_out, x2d, g1, seq, _tile(seq, 1024), _tile(d, 512))

    w_hi = w_router.astype(BF16)
    w_lo = (w_router - w_hi.astype(F32)).astype(BF16)
    wr_cat = jnp.pad(jnp.concatenate([w_hi, w_lo], axis=1), ((0, 0), (0, LANES - 2 * n_exp)))
    zeros = jnp.zeros_like(wr_cat)
    wr_pair = jnp.concatenate([jnp.concatenate([wr_cat, zeros], axis=1),
                               jnp.concatenate([zeros, wr_cat], axis=1)], axis=0)
    br_pad = jnp.pad(b_router.reshape(1, n_exp), ((0, 0), (0, LANES - n_exp)))
    hp, top_idx, top_w, rank_in_exp, counts = _router(x1, norm_ffn_g.reshape(1, d), sc2, sh2, wr_pair, br_pad,
                                                      seq, n_exp)

    counts = counts[0, :n_exp].astype(I32)
    padded = (counts + MOE_ROWS - 1) // MOE_ROWS * MOE_ROWS
    pad_end = jnp.cumsum(padded)
    pad_start = pad_end - padded
    top_idx = top_idx[:, :TOP_K]
    pos_flat = (pad_start[top_idx] + rank_in_exp[:, :TOP_K]).reshape(-1)
    n_rows = t * TOP_K + n_exp * MOE_ROWS
    n_blocks = n_rows // MOE_ROWS
    blk_start = jnp.arange(n_blocks, dtype=I32) * MOE_ROWS
    blk_exp = jnp.minimum(jnp.sum(blk_start[:, None] >= pad_end[None, :], axis=1), n_exp - 1).astype(I32)
    n_valid = (pad_end[-1:] // MOE_ROWS).astype(I32)
    blk_fill = jnp.clip((pad_start + counts)[blk_exp] - blk_start, 0, MOE_ROWS).astype(I32)

    xs = _dispatch(pad_end.astype(I32), padded.astype(I32), pos_flat, hp, n_rows)
    hid = _expert_up(blk_exp, n_valid, blk_fill, xs, w_exp_gate, b_exp_gate.reshape(n_exp, 1, f),
                     w_exp_up, b_exp_up.reshape(n_exp, 1, f), _tile(f, 512))
    tn_down = _tile(d, 4096)
    ys = _expert_down(blk_exp, n_valid, blk_fill, hid, w_exp_down, b_exp_down.reshape(n_exp, 1, d), tn_down)
    return _combine(pos_flat, top_w, x1, g2, final_g, ys, seq, tn_down)


def kernel(x, c, w_ada, b_ada, norm_mix_g, w_in, w_alpha_up, b_alpha, gla_norm_g, sgu_ln_g, sgu_ln_b,
           w_spatial, b_spatial, w_branch_a, w_branch_b, w_out, norm_ffn_g, w_router, b_router,
           w_exp_gate, b_exp_gate, w_exp_up, b_exp_up, w_exp_down, b_exp_down, norm_final_g):
    batch, seq, d = x.shape
    depth = w_ada.shape[0]
    assert depth == 1, "the final rmsnorm is fused into the last layer's combine kernel"
    c_pad = jnp.pad(c, ((0, 8 - batch % 8), (0, 0))) if batch % 8 else c
    out = _layer(x.reshape(batch * seq, d), c_pad, batch, seq, w_ada[0], b_ada[0], norm_mix_g[0], w_in[0],
                 w_alpha_up[0], b_alpha[0], gla_norm_g[0], sgu_ln_g[0], sgu_ln_b[0], w_spatial[0],
                 b_spatial[0], w_branch_a[0], w_branch_b[0], w_out[0], norm_ffn_g[0], w_router[0],
                 b_router[0], w_exp_gate, b_exp_gate[0], w_exp_up, b_exp_up[0], w_exp_down, b_exp_down[0],
                 norm_final_g.reshape(1, d))
    return out.reshape(batch, seq, d)
```

```python
import functools

import jax
import jax.numpy as jnp
from jax import lax
from jax.experimental import pallas as pl
from jax.experimental.pallas import tpu as pltpu

GLA_HEADS = 4
GLA_CHUNK = 64
GLA_TAU = 16.0
SGU_CHUNK = 128
TOP_K = 4
SWIGLU_ALPHA = 1.702
SWIGLU_LIMIT = 7.0
EPS = 1e-5

LANES = 128
MOE_ROWS = 512
VMEM_LIMIT = 56 * 1024 * 1024

F32 = jnp.float32
BF16 = jnp.bfloat16
U32 = jnp.uint32
I32 = jnp.int32


def _params(*sem):
    return pltpu.CompilerParams(dimension_semantics=sem, vmem_limit_bytes=VMEM_LIMIT)


def _tile(n, want):
    if n <= want:
        return n
    t = want
    while t >= LANES:
        if n % t == 0:
            return t
        t -= LANES
    return n


def _dot(a, b):
    return jnp.dot(a, b, preferred_element_type=F32)


def _dot_nt(a, b):
    return lax.dot_general(a, b, (((1,), (1,)), ((), ())), preferred_element_type=F32)


def _dot_tn(a, b):
    return lax.dot_general(a, b, (((0,), (0,)), ((), ())), preferred_element_type=F32)


def _sigmoid(x):
    return 0.5 * jnp.tanh(0.5 * x) + 0.5


def _pack_halves(y):
    half = y.shape[1] // 2
    bits = lax.bitcast_convert_type(y.astype(BF16).astype(F32), U32)
    return (bits[:, :half] >> 16) | bits[:, half:]


def _unpack_lo(p):
    return lax.bitcast_convert_type(p << 16, F32)


def _unpack_hi(p):
    return lax.bitcast_convert_type(p & jnp.uint32(0xFFFF0000), F32)


def _ada_kernel(c_ref, w_ref, b_ref, o_ref):
    c = c_ref[...]
    s = c * _sigmoid(c)
    o_ref[...] = _dot(s.astype(BF16), w_ref[...].astype(BF16)) + b_ref[...]


def _ada(c_pad, w, b):
    rows, d = c_pad.shape
    n = w.shape[1]
    tn = _tile(n, 512)
    return pl.pallas_call(
        _ada_kernel,
        grid=(n // tn,),
        in_specs=[pl.BlockSpec((rows, d), lambda j: (0, 0)),
                  pl.BlockSpec((d, tn), lambda j: (0, j)),
                  pl.BlockSpec((1, tn), lambda j: (0, j))],
        out_specs=pl.BlockSpec((rows, tn), lambda j: (0, j)),
        out_shape=jax.ShapeDtypeStruct((rows, n), F32),
        compiler_params=_params("arbitrary"),
        name="ada_mod",
    )(c_pad, w, b)


def _norm_mod_kernel(x_ref, g_ref, sc_ref, sh_ref, o_ref):
    x = x_ref[...]
    inv = lax.rsqrt(jnp.mean(x * x, axis=-1, keepdims=True) + EPS)
    h = (x * inv) * g_ref[...] * (1.0 + sc_ref[...]) + sh_ref[...]
    o_ref[...] = h.astype(o_ref.dtype)


def _norm_mod(x2d, g, sc, sh, seq):
    t, d = x2d.shape
    tm = _tile(seq, 512)
    bpb = seq // tm
    return pl.pallas_call(
        _norm_mod_kernel,
        grid=(t // tm,),
        in_specs=[pl.BlockSpec((tm, d), lambda i: (i, 0)),
                  pl.BlockSpec((1, d), lambda i: (0, 0)),
                  pl.BlockSpec((None, 1, d), lambda i: (i // bpb, 0, 0)),
                  pl.BlockSpec((None, 1, d), lambda i: (i // bpb, 0, 0))],
        out_specs=pl.BlockSpec((tm, d), lambda i: (i, 0)),
        out_shape=jax.ShapeDtypeStruct((t, d), BF16),
        compiler_params=_params("arbitrary"),
        name="norm_mod",
    )(x2d, g, sc, sh)


def _mm_act_kernel(a_ref, wt_ref, o_ref, *, act):
    acc = _dot_nt(a_ref[...], wt_ref[...].astype(BF16))
    if act == "gelu":
        acc = jax.nn.gelu(acc)
    elif act == "sigmoid":
        acc = _sigmoid(acc)
    o_ref[...] = acc.astype(o_ref.dtype)


def _matmul_act(a, w_t, tm, tn, row0, n_rows, act, name):
    m, k = a.shape
    assert n_rows % tn == 0 and row0 % 8 == 0
    return pl.pallas_call(
        functools.partial(_mm_act_kernel, act=act),
        grid=(m // tm, n_rows // tn),
        in_specs=[pl.BlockSpec((tm, k), lambda i, j: (i, 0)),
                  pl.BlockSpec((pl.Element(tn), pl.Element(k)),
                               lambda i, j: (pl.multiple_of(row0 + j * tn, 8), 0))],
        out_specs=pl.BlockSpec((tm, tn), lambda i, j: (i, j)),
        out_shape=jax.ShapeDtypeStruct((m, n_rows), BF16),
        compiler_params=_params("arbitrary", "arbitrary"),
        name=name,
    )(a, w_t)


def _alpha_kernel(h_ref, wt_ref, wup_ref, b_ref, o_ref, *, rank):
    a = _dot_nt(h_ref[...], wt_ref[...].astype(BF16))
    lane = lax.broadcasted_iota(I32, a.shape, 1)
    a = jnp.where(lane < rank, a, 0.0)
    z = _dot(a.astype(BF16), wup_ref[...]) + b_ref[...]
    log_sig = jnp.minimum(z, 0.0) - jnp.log1p(jnp.exp(-jnp.abs(z)))
    o_ref[...] = log_sig / GLA_TAU


def _alpha(h, w_t, row0, rank, wup_pad, b_alpha):
    t, d = h.shape
    key = wup_pad.shape[1]
    tm = _tile(t, 1024)
    assert row0 % 8 == 0 and row0 + LANES <= w_t.shape[0]
    return pl.pallas_call(
        functools.partial(_alpha_kernel, rank=rank),
        grid=(t // tm,),
        in_specs=[pl.BlockSpec((tm, d), lambda i: (i, 0)),
                  pl.BlockSpec((pl.Element(LANES), pl.Element(d)), lambda i: (row0, 0)),
                  pl.BlockSpec((LANES, key), lambda i: (0, 0)),
                  pl.BlockSpec((1, key), lambda i: (0, 0))],
        out_specs=pl.BlockSpec((tm, key), lambda i: (i, 0)),
        out_shape=jax.ShapeDtypeStruct((t, key), F32),
        compiler_params=_params("arbitrary"),
        name="gla_log_alpha",
    )(h, w_t, wup_pad, b_alpha)


def _gla_kernel(q_ref, k_ref, v_ref, la_ref, g_ref, gn_ref, o_ref, st_ref, *, n_chunks, scale):
    @pl.when(pl.program_id(1) == 0)
    def _():
        st_ref[...] = jnp.zeros_like(st_ref)

    c_len = GLA_CHUNK
    blk = n_chunks * c_len
    dk = q_ref.shape[1] // GLA_HEADS
    dv = v_ref.shape[1] // GLA_HEADS
    row = lax.broadcasted_iota(I32, (blk, blk), 0)
    col = lax.broadcasted_iota(I32, (blk, blk), 1)
    causal = col <= row
    tri = jnp.where(causal, 1.0, 0.0).astype(BF16)
    la = la_ref[...]
    la_hi = la.astype(BF16)
    la_lo = (la - la_hi.astype(F32)).astype(BF16)
    cum_all = _dot(tri, la_hi) + _dot(tri, la_lo)
    for hd in range(GLA_HEADS):
        ks = slice(hd * dk, (hd + 1) * dk)
        vs = slice(hd * dv, (hd + 1) * dv)
        cum = cum_all[:, ks]
        ends = [cum[(c + 1) * c_len - 1:(c + 1) * c_len, :] for c in range(n_chunks)]
        own_end = jnp.concatenate([jnp.broadcast_to(e, (c_len, dk)) for e in ends], axis=0)
        q = q_ref[:, ks].astype(F32) * scale
        k = k_ref[:, ks].astype(F32)
        v = v_ref[:, vs]
        k_rel = (k * jnp.exp(own_end - cum)).astype(BF16)
        cols = []
        for c in range(n_chunks):
            lo = c * c_len
            q_c = (q[lo:] * jnp.exp(cum[lo:] - ends[c])).astype(BF16)
            s_c = _dot_nt(q_c, k_rel[lo:lo + c_len, :])
            cols.append(s_c if c == 0 else jnp.concatenate([jnp.zeros((lo, c_len), F32), s_c], axis=0))
        scores = jnp.where(causal, jnp.concatenate(cols, axis=1), 0.0).astype(BF16)
        st = st_ref[hd]
        q_abs = (q * jnp.exp(cum)).astype(BF16)
        o = _dot(scores, v) + _dot_nt(q_abs, st.astype(BF16))
        k_end = (k * jnp.exp(ends[-1] - cum)).astype(BF16)
        st_ref[hd] = st * jnp.exp(ends[-1]) + _dot_tn(v, k_end)
        inv = lax.rsqrt(jnp.mean(o * o, axis=-1, keepdims=True) + EPS)
        g = g_ref[:, vs].astype(F32)
        o_ref[:, vs] = (o * inv * gn_ref[...] * (g * _sigmoid(g))).astype(o_ref.dtype)


def _gla(qkvg, log_a, gn, batch, seq, key, val):
    t = qkvg.shape[0]
    dk, dv = key // GLA_HEADS, val // GLA_HEADS
    blk = _tile(seq, 256)
    assert blk % GLA_CHUNK == 0 and (2 * key) % val == 0
    nb = seq // blk
    row = lambda b, n: b * nb + n
    return pl.pallas_call(
        functools.partial(_gla_kernel, n_chunks=blk // GLA_CHUNK, scale=float(dk) ** -0.5),
        grid=(batch, nb),
        in_specs=[pl.BlockSpec((blk, key), lambda b, n: (row(b, n), 0)),
                  pl.BlockSpec((blk, key), lambda b, n: (row(b, n), 1)),
                  pl.BlockSpec((blk, val), lambda b, n: (row(b, n), 2 * key // val)),
                  pl.BlockSpec((blk, key), lambda b, n: (row(b, n), 0)),
                  pl.BlockSpec((blk, val), lambda b, n: (row(b, n), 2 * key // val + 1)),
                  pl.BlockSpec((1, dv), lambda b, n: (0, 0))],
        out_specs=pl.BlockSpec((blk, val), lambda b, n: (row(b, n), 0)),
        out_shape=jax.ShapeDtypeStruct((t, val), BF16),
        scratch_shapes=[pltpu.VMEM((GLA_HEADS, dv, dk), F32)],
        compiler_params=_params("arbitrary", "arbitrary"),
        name="gla_mixer",
    )(qkvg, qkvg, qkvg, log_a, qkvg, gn)


def _sgu_kernel(u_ref, z_ref, lg_ref, lb_ref, w_ref, bs_ref, o_ref, *, n_chunks, groups):
    c_len = SGU_CHUNK
    gd = u_ref.shape[1] // groups
    row = lax.broadcasted_iota(I32, (c_len, c_len), 0)
    col = lax.broadcasted_iota(I32, (c_len, c_len), 1)
    causal = col <= row
    w_masked = [jnp.where(causal, w_ref[g], 0.0).astype(BF16) for g in range(groups)]
    for c in range(n_chunks):
        sl = slice(c * c_len, (c + 1) * c_len)
        z = z_ref[sl, :].astype(F32)
        mu = jnp.mean(z, axis=-1, keepdims=True)
        zc = z - mu
        var = jnp.mean(zc * zc, axis=-1, keepdims=True)
        zn = (zc * lax.rsqrt(var + EPS) * lg_ref[...] + lb_ref[...]).astype(BF16)
        for g in range(groups):
            gs = slice(g * gd, (g + 1) * gd)
            s = _dot(w_masked[g], zn[:, gs]) + bs_ref[:, gs]
            o_ref[sl, gs] = (u_ref[sl, gs].astype(F32) * s).astype(o_ref.dtype)


def _sgu(uz, ln_g, ln_b, w_spatial, bs_full, seq, width):
    t = uz.shape[0]
    groups = w_spatial.shape[0]
    blk = _tile(seq, 512)
    assert blk % SGU_CHUNK == 0
    return pl.pallas_call(
        functools.partial(_sgu_kernel, n_chunks=blk // SGU_CHUNK, groups=groups),
        grid=(t // blk,),
        in_specs=[pl.BlockSpec((blk, width), lambda i: (i, 0)),
                  pl.BlockSpec((blk, width), lambda i: (i, 1)),
                  pl.BlockSpec((1, width), lambda i: (0, 0)),
                  pl.BlockSpec((1, width), lambda i: (0, 0)),
                  pl.BlockSpec((groups, SGU_CHUNK, SGU_CHUNK), lambda i: (0, 0, 0)),
                  pl.BlockSpec((SGU_CHUNK, width), lambda i: (0, 0))],
        out_specs=pl.BlockSpec((blk, width), lambda i: (i, 0)),
        out_shape=jax.ShapeDtypeStruct((t, width), BF16),
        compiler_params=_params("arbitrary"),
        name="sgu_mixer",
    )(uz, uz, ln_g, ln_b, w_spatial, bs_full)


def _merge_kernel(a_ref, b_ref, wa_ref, wb_ref, ga_ref, gb_ref, o_ref):
    ya = _dot(a_ref[...], wa_ref[...].astype(BF16))
    yb = _dot(b_ref[...], wb_ref[...].astype(BF16))
    o_ref[...] = (ga_ref[...].astype(F32) * ya + gb_ref[...].astype(F32) * yb).astype(o_ref.dtype)


def _merge(a, b, wa, wb, gates, tm, tn):
    t, val = a.shape
    width = b.shape[1]
    d = wa.shape[1]
    return pl.pallas_call(
        _merge_kernel,
        grid=(t // tm, d // tn),
        in_specs=[pl.BlockSpec((tm, val), lambda i, j: (i, 0)),
                  pl.BlockSpec((tm, width), lambda i, j: (i, 0)),
                  pl.BlockSpec((val, tn), lambda i, j: (0, j)),
                  pl.BlockSpec((width, tn), lambda i, j: (0, j)),
                  pl.BlockSpec((tm, tn), lambda i, j: (i, j)),
                  pl.BlockSpec((tm, tn), lambda i, j: (i, d // tn + j))],
        out_specs=pl.BlockSpec((tm, tn), lambda i, j: (i, j)),
        out_shape=jax.ShapeDtypeStruct((t, d), BF16),
        compiler_params=_params("arbitrary", "arbitrary"),
        name="merge_branches",
    )(a, b, wa, wb, gates, gates)


def _outproj_kernel(m_ref, w_ref, x_ref, g1_ref, o_ref):
    y = _dot(m_ref[...], w_ref[...].astype(BF16))
    o_ref[...] = x_ref[...] + g1_ref[...] * y


def _outproj(merged, w, x2d, g1, seq, tm, tn):
    t, d = x2d.shape
    bpb = seq // tm
    return pl.pallas_call(
        _outproj_kernel,
        grid=(t // tm, d // tn),
        in_specs=[pl.BlockSpec((tm, d), lambda i, j: (i, 0)),
                  pl.BlockSpec((d, tn), lambda i, j: (0, j)),
                  pl.BlockSpec((tm, tn), lambda i, j: (i, j)),
                  pl.BlockSpec((None, 1, tn), lambda i, j: (i // bpb, 0, j))],
        out_specs=pl.BlockSpec((tm, tn), lambda i, j: (i, j)),
        out_shape=jax.ShapeDtypeStruct((t, d), F32),
        compiler_params=_params("arbitrary", "arbitrary"),
        name="out_proj_residual",
    )(merged, w, x2d, g1)


def _router_kernel(x_ref, g_ref, sc_ref, sh_ref, wr_ref, br_ref,
                   hp_ref, idx_ref, wt_ref, rank_ref, cnt_ref, carry_ref, *, n_exp):
    @pl.when(pl.program_id(0) == 0)
    def _():
        carry_ref[...] = jnp.zeros_like(carry_ref)

    x = x_ref[...]
    tm = x.shape[0]
    inv = lax.rsqrt(jnp.mean(x * x, axis=-1, keepdims=True) + EPS)
    h = (x * inv) * g_ref[...] * (1.0 + sc_ref[...]) + sh_ref[...]
    hp_ref[...] = _pack_halves(h)
    h_hi = h.astype(BF16)
    h_lo = (h - h_hi.astype(F32)).astype(BF16)
    def router_dot(a):
        pair = jnp.concatenate([a[:tm // 2], a[tm // 2:]], axis=1)
        out = _dot(pair, wr_ref[...])
        return jnp.concatenate([out[:, :LANES], out[:, LANES:]], axis=0)

    p = router_dot(h_hi)
    q = router_dot(h_lo)
    lane = lax.broadcasted_iota(I32, (tm, LANES), 1)
    logits = p + pltpu.roll(p, LANES - n_exp, 1) + q + br_ref[...]
    neg_inf = jnp.float32(-jnp.inf)
    work = jnp.where(lane < n_exp, logits, neg_inf)
    vals, sels = [], []
    idx_out = jnp.zeros((tm, LANES), I32)
    for k in range(TOP_K):
        m = jnp.max(work, axis=-1, keepdims=True)
        idx = jnp.min(jnp.where(work == m, lane, LANES), axis=-1, keepdims=True)
        sel = lane == idx
        work = jnp.where(sel, neg_inf, work)
        vals.append(m)
        sels.append(sel)
        idx_out = jnp.where(lane == k, idx, idx_out)
    exps = [jnp.exp(v - vals[0]) for v in vals]
    denom = exps[0]
    for e in exps[1:]:
        denom = denom + e
    wt_out = jnp.zeros((tm, LANES), F32)
    for k in range(TOP_K):
        wt_out = jnp.where(lane == k, exps[k] / denom, wt_out)
    onehot = jnp.zeros((tm, LANES), F32)
    for sel in sels:
        onehot = jnp.where(sel, 1.0, onehot)
    r = lax.broadcasted_iota(I32, (tm, tm), 0)
    c = lax.broadcasted_iota(I32, (tm, tm), 1)
    strict = jnp.where(c < r, 1.0, 0.0).astype(BF16)
    before = carry_ref[...] + _dot(strict, onehot.astype(BF16))
    rank_out = jnp.zeros((tm, LANES), I32)
    for k in range(TOP_K):
        rk = jnp.sum(jnp.where(sels[k], before, 0.0), axis=-1, keepdims=True)
        rank_out = jnp.where(lane == k, rk.astype(I32), rank_out)
    carry_ref[...] = carry_ref[...] + jnp.sum(onehot, axis=0, keepdims=True)
    idx_ref[...] = idx_out
    wt_ref[...] = wt_out
    rank_ref[...] = rank_out
    cnt_ref[...] = carry_ref[...]


def _router(x1, g, sc, sh, wr_cat, br_pad, seq, n_exp):
    t, d = x1.shape
    tm = _tile(seq, 256)
    bpb = seq // tm
    tok = pl.BlockSpec((tm, LANES), lambda i: (i, 0))
    return pl.pallas_call(
        functools.partial(_router_kernel, n_exp=n_exp),
        grid=(t // tm,),
        in_specs=[pl.BlockSpec((tm, d), lambda i: (i, 0)),
                  pl.BlockSpec((1, d), lambda i: (0, 0)),
                  pl.BlockSpec((None, 1, d), lambda i: (i // bpb, 0, 0)),
                  pl.BlockSpec((None, 1, d), lambda i: (i // bpb, 0, 0)),
                  pl.BlockSpec((2 * d, 2 * LANES), lambda i: (0, 0)),
                  pl.BlockSpec((1, LANES), lambda i: (0, 0))],
        out_specs=[pl.BlockSpec((tm, d // 2), lambda i: (i, 0)), tok, tok, tok,
                   pl.BlockSpec((1, LANES), lambda i: (0, 0))],
        out_shape=[jax.ShapeDtypeStruct((t, d // 2), U32),
                   jax.ShapeDtypeStruct((t, LANES), I32),
                   jax.ShapeDtypeStruct((t, LANES), F32),
                   jax.ShapeDtypeStruct((t, LANES), I32),
                   jax.ShapeDtypeStruct((1, LANES), F32)],
        scratch_shapes=[pltpu.VMEM((1, LANES), F32)],
        compiler_params=_params("arbitrary"),
        name="norm_router_topk",
    )(x1, g, sc, sh, wr_cat, br_pad)


def _dispatch_kernel(pad_end_ref, padded_ref, pos_ref, hp_ref, xs_ref, zero_ref, zsem, sem, *, tm, n_exp):
    @pl.when(pl.program_id(0) == 0)
    def _():
        zero_ref[...] = jnp.zeros_like(zero_ref)

        def zero_block(b):
            start = pl.multiple_of(b * MOE_ROWS, MOE_ROWS)
            return pltpu.make_async_copy(zero_ref, xs_ref.at[pl.ds(start, MOE_ROWS)], zsem)

        def start_tail(b, carry):
            zero_block(b).start()
            return carry

        def wait_tail(b, carry):
            zero_block(b).wait()
            return carry

        n_used = pad_end_ref[n_exp - 1] // MOE_ROWS
        n_blocks = xs_ref.shape[0] // MOE_ROWS
        for e in range(n_exp):
            @pl.when(padded_ref[e] > 0)
            def _():
                zero_block(pad_end_ref[e] // MOE_ROWS - 1).start()
        lax.fori_loop(n_used, n_blocks, start_tail, 0)
        for e in range(n_exp):
            @pl.when(padded_ref[e] > 0)
            def _():
                zero_block(pad_end_ref[e] // MOE_ROWS - 1).wait()
        lax.fori_loop(n_used, n_blocks, wait_tail, 0)

    def row_copy(r, p):
        return pltpu.make_async_copy(hp_ref.at[pl.ds(r, 1)], xs_ref.at[pl.ds(p, 1)], sem)

    def start(r, carry):
        for k in range(TOP_K):
            row_copy(r, pos_ref[r * TOP_K + k]).start(priority=k % 2)
        return carry

    lax.fori_loop(0, tm, start, 0)
    for _ in range(TOP_K):
        pltpu.make_async_copy(hp_ref, xs_ref.at[pl.ds(0, tm)], sem).wait()


def _dispatch(pad_end, padded, pos_flat, hp, n_rows):
    t, half = hp.shape
    tm = _tile(t, 512)
    n_exp = pad_end.shape[0]
    grid_spec = pltpu.PrefetchScalarGridSpec(
        num_scalar_prefetch=2,
        grid=(t // tm,),
        in_specs=[pl.BlockSpec((tm * TOP_K,), lambda i, pe, pd: (i,), memory_space=pltpu.SMEM),
                  pl.BlockSpec((tm, half), lambda i, pe, pd: (i, 0))],
        out_specs=pl.BlockSpec(memory_space=pl.ANY),
        scratch_shapes=[pltpu.VMEM((MOE_ROWS, half), U32),
                        pltpu.SemaphoreType.DMA(()), pltpu.SemaphoreType.DMA(())],
    )
    return pl.pallas_call(
        functools.partial(_dispatch_kernel, tm=tm, n_exp=n_exp),
        grid_spec=grid_spec,
        out_shape=jax.ShapeDtypeStruct((n_rows, half), U32),
        compiler_params=_params("arbitrary"),
        name="moe_dispatch",
    )(pad_end, padded, pos_flat, hp)


def _weight_runs(blk_exp, n_outer):
    n_blocks = blk_exp.shape[0]
    total = n_outer * n_blocks
    e_flat = jnp.tile(blk_exp, n_outer)
    j_flat = jnp.repeat(jnp.arange(n_outer, dtype=I32), n_blocks)
    changed = (e_flat[1:] != e_flat[:-1]) | (j_flat[1:] != j_flat[:-1])
    first = jnp.concatenate([jnp.ones((1,), bool), changed])
    slot = (jnp.cumsum(first.astype(I32)) - 1) % 2
    steps = jnp.arange(total, dtype=I32)
    nxt = lax.cummin(jnp.where(first, steps, total)[::-1])[::-1]
    nxt = jnp.concatenate([nxt[1:], jnp.full((1,), total, I32)])
    has_next = nxt < total
    nxt_c = jnp.minimum(nxt, total - 1)
    return (first.astype(I32), slot.astype(I32), has_next.astype(I32), e_flat[nxt_c].astype(I32),
            j_flat[nxt_c].astype(I32))


def _for_block_fill(fill, compute, out_ref):
    quarter = MOE_ROWS // 4
    for rows in range(quarter, MOE_ROWS + 1, quarter):
        @pl.when(jnp.logical_and(fill > rows - quarter, fill <= rows))
        def _(rows=rows):
            compute(rows)
            if rows < MOE_ROWS:
                out_ref[rows:, :] = jnp.zeros((MOE_ROWS - rows, out_ref.shape[1]), out_ref.dtype)

    @pl.when(fill == 0)
    def _():
        out_ref[...] = jnp.zeros_like(out_ref)


def _expert_up_kernel(be_ref, nv_ref, fill_ref, first_ref, slot_ref, pf_ok_ref, pf_e_ref, pf_j_ref,
                      x_ref, bg_ref, bu_ref, wg_hbm, wu_hbm, h_ref, stage_ref, sem, *, tf):
    del nv_ref
    j, i = pl.program_id(0), pl.program_id(1)
    s = j * pl.num_programs(1) + i
    slot = slot_ref[s]

    def w_copies(e, jj, sl):
        cols = pl.ds(pl.multiple_of(jj * tf, tf), tf)
        return [pltpu.make_async_copy(w.at[0, e, :, cols], stage_ref.at[sl, t], sem.at[sl, t])
                for t, w in enumerate((wg_hbm, wu_hbm))]

    @pl.when(s == 0)
    def _():
        for cp in w_copies(be_ref[0], 0, 0):
            cp.start()

    @pl.when(first_ref[s] == 1)
    def _():
        for cp in w_copies(be_ref[i], j, slot):
            cp.wait()

        @pl.when(pf_ok_ref[s] == 1)
        def _():
            for cp in w_copies(pf_e_ref[s], pf_j_ref[s], 1 - slot):
                cp.start()

    def compute(rows):
        xp = x_ref[:rows, :]
        half = xp.shape[1]
        x_lo = _unpack_lo(xp)
        x_hi = _unpack_hi(xp)
        wg_lo, wg_hi = stage_ref[slot, 0, :half, :], stage_ref[slot, 0, half:, :]
        wu_lo, wu_hi = stage_ref[slot, 1, :half, :], stage_ref[slot, 1, half:, :]
        glu = _dot(x_lo, wg_lo) + _dot(x_hi, wg_hi) + bg_ref[...]
        lin = _dot(x_lo, wu_lo) + _dot(x_hi, wu_hi) + bu_ref[...]
        glu = jnp.minimum(glu, SWIGLU_LIMIT)
        lin = jnp.clip(lin, -SWIGLU_LIMIT, SWIGLU_LIMIT)
        h_ref[:rows, :] = (glu * _sigmoid(SWIGLU_ALPHA * glu) * (lin + 1.0)).astype(h_ref.dtype)

    _for_block_fill(fill_ref[i], compute, h_ref)


def _expert_up(blk_exp, n_valid, blk_fill, xs, wg, bg, wu, bu, tf):
    n_rows, half = xs.shape
    d, f = wg.shape[-2], wg.shape[-1]
    n_blocks = n_rows // MOE_ROWS
    runs = _weight_runs(blk_exp, f // tf)
    last = lambda i, nv: jnp.minimum(i, nv[0] - 1)
    grid_spec = pltpu.PrefetchScalarGridSpec(
        num_scalar_prefetch=3 + len(runs),
        grid=(f // tf, n_blocks),
        in_specs=[pl.BlockSpec((MOE_ROWS, half), lambda j, i, be, nv, *_: (last(i, nv), 0)),
                  pl.BlockSpec((None, 1, tf), lambda j, i, be, nv, *_: (be[i], 0, j)),
                  pl.BlockSpec((None, 1, tf), lambda j, i, be, nv, *_: (be[i], 0, j)),
                  pl.BlockSpec(memory_space=pl.ANY),
                  pl.BlockSpec(memory_space=pl.ANY)],
        out_specs=pl.BlockSpec((MOE_ROWS, tf), lambda j, i, be, nv, *_: (i, j)),
        scratch_shapes=[pltpu.VMEM((2, 2, d, tf), F32), pltpu.SemaphoreType.DMA((2, 2))],
    )
    return pl.pallas_call(
        functools.partial(_expert_up_kernel, tf=tf),
        grid_spec=grid_spec,
        out_shape=jax.ShapeDtypeStruct((n_rows, f), F32),
        compiler_params=_params("arbitrary", "arbitrary"),
        name="expert_up",
    )(blk_exp, n_valid, blk_fill, *runs, xs, bg, bu, wg, wu)


def _expert_down_kernel(be_ref, nv_ref, fill_ref, first_ref, slot_ref, pf_ok_ref, pf_e_ref, pf_j_ref,
                        h_ref, bd_ref, wd_hbm, o_ref, stage_ref, sem, *, tn):
    del nv_ref
    j, i = pl.program_id(0), pl.program_id(1)
    s = j * pl.num_programs(1) + i
    slot = slot_ref[s]

    def w_copy(e, jj, sl):
        cols = pl.ds(pl.multiple_of(jj * tn, tn), tn)
        return pltpu.make_async_copy(wd_hbm.at[0, e, :, cols], stage_ref.at[sl], sem.at[sl])

    @pl.when(s == 0)
    def _():
        w_copy(be_ref[0], 0, 0).start()

    @pl.when(first_ref[s] == 1)
    def _():
        w_copy(be_ref[i], j, slot).wait()

        @pl.when(pf_ok_ref[s] == 1)
        def _():
            w_copy(pf_e_ref[s], pf_j_ref[s], 1 - slot).start()

    def compute(rows):
        y = _dot(h_ref[:rows, :], stage_ref[slot]) + bd_ref[...]
        o_ref[:rows, :] = _pack_halves(y)

    _for_block_fill(fill_ref[i], compute, o_ref)


def _expert_down(blk_exp, n_valid, blk_fill, hid, wd, bd, tn):
    n_rows, f = hid.shape
    d = wd.shape[-1]
    n_blocks = n_rows // MOE_ROWS
    runs = _weight_runs(blk_exp, d // tn)
    last = lambda i, nv: jnp.minimum(i, nv[0] - 1)
    grid_spec = pltpu.PrefetchScalarGridSpec(
        num_scalar_prefetch=3 + len(runs),
        grid=(d // tn, n_blocks),
        in_specs=[pl.BlockSpec((MOE_ROWS, f), lambda j, i, be, nv, *_: (last(i, nv), 0)),
                  pl.BlockSpec((None, 1, tn), lambda j, i, be, nv, *_: (be[i], 0, j)),
                  pl.BlockSpec(memory_space=pl.ANY)],
        out_specs=pl.BlockSpec((MOE_ROWS, tn // 2), lambda j, i, be, nv, *_: (i, j)),
        scratch_shapes=[pltpu.VMEM((2, f, tn), F32), pltpu.SemaphoreType.DMA((2,))],
    )
    return pl.pallas_call(
        functools.partial(_expert_down_kernel, tn=tn),
        grid_spec=grid_spec,
        out_shape=jax.ShapeDtypeStruct((n_rows, d // 2), U32),
        compiler_params=_params("arbitrary", "arbitrary"),
        name="expert_down",
    )(blk_exp, n_valid, blk_fill, *runs, hid, bd, wd)


def _combine_kernel(pos_ref, pos_next_ref, wt_ref, x_ref, g2_ref, gf_ref, ys_ref, o_ref, buf_ref, sem,
                    *, tm, tn):
    i = pl.program_id(0)
    slot = i % 2

    def row_copy(sl, r, k, p):
        return pltpu.make_async_copy(ys_ref.at[pl.ds(p, 1)], buf_ref.at[sl, k, pl.ds(r, 1)], sem.at[sl])

    def issue(p_ref, sl):
        def body(r, carry):
            for k in range(TOP_K):
                row_copy(sl, r, k, p_ref[r * TOP_K + k]).start(priority=k % 2)
            return carry
        lax.fori_loop(0, tm, body, 0)

    @pl.when(i == 0)
    def _():
        issue(pos_ref, 0)

    @pl.when(i + 1 < pl.num_programs(0))
    def _():
        issue(pos_next_ref, 1 - slot)

    for k in range(TOP_K):
        pltpu.make_async_copy(ys_ref.at[pl.ds(0, tm)], buf_ref.at[slot, k], sem.at[slot]).wait()

    d = x_ref.shape[1]
    hw = tn // 2
    wts = [wt_ref[:, k:k + 1] for k in range(TOP_K)]
    pieces = []
    ssq = jnp.zeros((tm, 1), F32)
    for j in range(d // tn):
        for part, unpack in ((0, _unpack_lo), (1, _unpack_hi)):
            cols = slice(j * tn + part * hw, j * tn + (part + 1) * hw)
            y = jnp.zeros((tm, hw), F32)
            for k in range(TOP_K):
                y = y + wts[k] * unpack(buf_ref[slot, k, :, j * hw:(j + 1) * hw])
            x2 = x_ref[:, cols] + g2_ref[:, cols] * y
            ssq = ssq + jnp.sum(x2 * x2, axis=-1, keepdims=True)
            pieces.append((cols, x2))
    inv = lax.rsqrt(ssq / d + EPS)
    for cols, x2 in pieces:
        o_ref[:, cols] = x2 * inv * gf_ref[:, cols]


def _combine(pos_flat, wt, x1, g2, gf, ys, seq, tn):
    t, d = x1.shape
    tm = _tile(seq, 256)
    bpb = seq // tm
    n_steps = t // tm
    return pl.pallas_call(
        functools.partial(_combine_kernel, tm=tm, tn=tn),
        grid=(n_steps,),
        in_specs=[pl.BlockSpec((tm * TOP_K,), lambda i: (i,), memory_space=pltpu.SMEM),
                  pl.BlockSpec((tm * TOP_K,), lambda i: (jnp.minimum(i + 1, n_steps - 1),),
                               memory_space=pltpu.SMEM),
                  pl.BlockSpec((tm, LANES), lambda i: (i, 0)),
                  pl.BlockSpec((tm, d), lambda i: (i, 0)),
                  pl.BlockSpec((None, 1, d), lambda i: (i // bpb, 0, 0)),
                  pl.BlockSpec((1, d), lambda i: (0, 0)),
                  pl.BlockSpec(memory_space=pl.ANY)],
        out_specs=pl.BlockSpec((tm, d), lambda i: (i, 0)),
        out_shape=jax.ShapeDtypeStruct((t, d), F32),
        scratch_shapes=[pltpu.VMEM((2, TOP_K, tm, d // 2), U32), pltpu.SemaphoreType.DMA((2,))],
        compiler_params=_params("arbitrary"),
        name="moe_combine_final_norm",
    )(pos_flat, pos_flat, wt, x1, g2, gf, ys)


def _layer(x2d, c_pad, batch, seq, w_ada, b_ada, norm_mix_g, w_in, w_alpha_up, b_alpha, gla_norm_g, sgu_ln_g,
           sgu_ln_b, w_spatial, b_spatial, w_branch_a, w_branch_b, w_out, norm_ffn_g, w_router, b_router,
           w_exp_gate, b_exp_gate, w_exp_up, b_exp_up, w_exp_down, b_exp_down, final_g):
    t, d = x2d.shape
    rank, key = w_alpha_up.shape
    val = w_branch_a.shape[0]
    width = w_branch_b.shape[0]
    n_exp = w_router.shape[1]
    f = w_exp_gate.shape[-1]
    assert 2 * n_exp <= LANES and rank <= LANES

    mod = _ada(c_pad, w_ada, b_ada.reshape(1, -1))[:batch].reshape(batch, 6, 1, d)
    sh1, sc1, g1, sh2, sc2, g2 = (mod[:, i] for i in range(6))

    h = _norm_mod(x2d, norm_mix_g.reshape(1, d), sc1, sh1, seq)

    tm = _tile(t, 1024)
    tn_dense = 512
    n_qkvg = 2 * key + 2 * val
    w_in_t = w_in.T
    qkvg = _matmul_act(h, w_in_t, tm, tn_dense, 0, n_qkvg, None, "proj_qkvg")
    wup = jnp.pad(w_alpha_up, ((0, LANES - rank), (0, 0))).astype(BF16)
    log_a = _alpha(h, w_in_t, n_qkvg, rank, wup, b_alpha.reshape(1, key))
    uz = _matmul_act(h, w_in_t, tm, tn_dense, n_qkvg + rank, 2 * width, "gelu", "proj_uz")
    gates = _matmul_act(h, w_in_t, tm, tn_dense, n_qkvg + rank + 2 * width, 2 * d, "sigmoid", "proj_gates")

    a = _gla(qkvg, log_a, gla_norm_g.reshape(1, -1), batch, seq, key, val)
    bs_full = jnp.repeat(b_spatial.T, width // w_spatial.shape[0], axis=1)
    b = _sgu(uz, sgu_ln_g.reshape(1, width), sgu_ln_b.reshape(1, width), w_spatial, bs_full, seq, width)

    merged = _merge(a, b, w_branch_a, w_branch_b, gates, _tile(t, 2048), 256)
    x1 = _outproj(merged, w_out, x2d, g1, seq, _tile(seq, 1024), _tile(d, 512))

    w_hi = w_router.astype(BF16)
    w_lo = (w_router - w_hi.astype(F32)).astype(BF16)
    wr_cat = jnp.pad(jnp.concatenate([w_hi, w_lo], axis=1), ((0, 0), (0, LANES - 2 * n_exp)))
    zeros = jnp.zeros_like(wr_cat)
    wr_pair = jnp.concatenate([jnp.concatenate([wr_cat, zeros], axis=1),
                               jnp.concatenate([zeros, wr_cat], axis=1)], axis=0)
    br_pad = jnp.pad(b_router.reshape(1, n_exp), ((0, 0), (0, LANES - n_exp)))
    hp, top_idx, top_w, rank_in_exp, counts = _router(x1, norm_ffn_g.reshape(1, d), sc2, sh2, wr_pair, br_pad,
                                                      seq, n_exp)

    counts = counts[0, :n_exp].astype(I32)
    padded = (counts + MOE_ROWS - 1) // MOE_ROWS * MOE_ROWS
    pad_end = jnp.cumsum(padded)
    pad_start = pad_end - padded
    top_idx = top_idx[:, :TOP_K]
    pos_flat = (pad_start[top_idx] + rank_in_exp[:, :TOP_K]).reshape(-1)
    n_rows = t * TOP_K + n_exp * MOE_ROWS
    n_blocks = n_rows // MOE_ROWS
    blk_start = jnp.arange(n_blocks, dtype=I32) * MOE_ROWS
    blk_exp = jnp.minimum(jnp.sum(blk_start[:, None] >= pad_end[None, :], axis=1), n_exp - 1).astype(I32)
    n_valid = (pad_end[-1:] // MOE_ROWS).astype(I32)
    blk_fill = jnp.clip((pad_start + counts)[blk_exp] - blk_start, 0, MOE_ROWS).astype(I32)

    xs = _dispatch(pad_end.astype(I32), padded.astype(I32), pos_flat, hp, n_rows)
    hid = _expert_up(blk_exp, n_valid, blk_fill, xs, w_exp_gate, b_exp_gate.reshape(n_exp, 1, f),
                     w_exp_up, b_exp_up.reshape(n_exp, 1, f), _tile(f, 512))
    tn_down = _tile(d, 4096)
    ys = _expert_down(blk_exp, n_valid, blk_fill, hid, w_exp_down, b_exp_down.reshape(n_exp, 1, d), tn_down)
    return _combine(pos_flat, top_w, x1, g2, final_g, ys, seq, tn_down)


def kernel(x, c, w_ada, b_ada, norm_mix_g, w_in, w_alpha_up, b_alpha, gla_norm_g, sgu_ln_g, sgu_ln_b,
           w_spatial, b_spatial, w_branch_a, w_branch_b, w_out, norm_ffn_g, w_router, b_router,
           w_exp_gate, b_exp_gate, w_exp_up, b_exp_up, w_exp_down, b_exp_down, norm_final_g):
    batch, seq, d = x.shape
    depth = w_ada.shape[0]
    assert depth == 1, "the final rmsnorm is fused into the last layer's combine kernel"
    c_pad = jnp.pad(c, ((0, 8 - batch % 8), (0, 0))) if batch % 8 else c
    out = _layer(x.reshape(batch * seq, d), c_pad, batch, seq, w_ada[0], b_ada[0], norm_mix_g[0], w_in[0],
                 w_alpha_up[0], b_alpha[0], gla_norm_g[0], sgu_ln_g[0], sgu_ln_b[0], w_spatial[0],
                 b_spatial[0], w_branch_a[0], w_branch_b[0], w_out[0], norm_ffn_g[0], w_router[0],
                 b_router[0], w_exp_gate, b_exp_gate[0], w_exp_up, b_exp_up[0], w_exp_down, b_exp_down[0],
                 norm_final_g.reshape(1, d))
    return out.reshape(batch, seq, d)
```

```python
import functools

import jax
import jax.numpy as jnp
from jax import lax
from jax.experimental import pallas as pl
from jax.experimental.pallas import tpu as pltpu

GLA_HEADS = 4
GLA_CHUNK = 64
GLA_TAU = 16.0
SGU_CHUNK = 128
TOP_K = 4
SWIGLU_ALPHA = 1.702
SWIGLU_LIMIT = 7.0
EPS = 1e-5

LANES = 128
MOE_ROWS = 512
VMEM_LIMIT = 56 * 1024 * 1024

F32 = jnp.float32
BF16 = jnp.bfloat16
U32 = jnp.uint32
I32 = jnp.int32


def _params(*sem):
    return pltpu.CompilerParams(dimension_semantics=sem, vmem_limit_bytes=VMEM_LIMIT)


def _tile(n, want):
    if n <= want:
        return n
    t = want
    while t >= LANES:
        if n % t == 0:
            return t
        t -= LANES
    return n


def _dot(a, b):
    return jnp.dot(a, b, preferred_element_type=F32)


def _dot_nt(a, b):
    return lax.dot_general(a, b, (((1,), (1,)), ((), ())), preferred_element_type=F32)


def _dot_tn(a, b):
    return lax.dot_general(a, b, (((0,), (0,)), ((), ())), preferred_element_type=F32)


def _sigmoid(x):
    return 0.5 * jnp.tanh(0.5 * x) + 0.5


def _pack_halves(y):
    half = y.shape[1] // 2
    bits = lax.bitcast_convert_type(y.astype(BF16).astype(F32), U32)
    return (bits[:, :half] >> 16) | bits[:, half:]


def _unpack_lo(p):
    return lax.bitcast_convert_type(p << 16, F32)


def _unpack_hi(p):
    return lax.bitcast_convert_type(p & jnp.uint32(0xFFFF0000), F32)


def _ada_kernel(c_ref, w_ref, b_ref, o_ref):
    c = c_ref[...]
    s = c * _sigmoid(c)
    o_ref[...] = _dot(s.astype(BF16), w_ref[...].astype(BF16)) + b_ref[...]


def _ada(c_pad, w, b):
    rows, d = c_pad.shape
    n = w.shape[1]
    tn = _tile(n, 512)
    return pl.pallas_call(
        _ada_kernel,
        grid=(n // tn,),
        in_specs=[pl.BlockSpec((rows, d), lambda j: (0, 0)),
                  pl.BlockSpec((d, tn), lambda j: (0, j)),
                  pl.BlockSpec((1, tn), lambda j: (0, j))],
        out_specs=pl.BlockSpec((rows, tn), lambda j: (0, j)),
        out_shape=jax.ShapeDtypeStruct((rows, n), F32),
        compiler_params=_params("arbitrary"),
        name="ada_mod",
    )(c_pad, w, b)


def _norm_mod_kernel(x_ref, g_ref, sc_ref, sh_ref, o_ref):
    x = x_ref[...]
    inv = lax.rsqrt(jnp.mean(x * x, axis=-1, keepdims=True) + EPS)
    h = (x * inv) * g_ref[...] * (1.0 + sc_ref[...]) + sh_ref[...]
    o_ref[...] = h.astype(o_ref.dtype)


def _norm_mod(x2d, g, sc, sh, seq):
    t, d = x2d.shape
    tm = _tile(seq, 512)
    bpb = seq // tm
    return pl.pallas_call(
        _norm_mod_kernel,
        grid=(t // tm,),
        in_specs=[pl.BlockSpec((tm, d), lambda i: (i, 0)),
                  pl.BlockSpec((1, d), lambda i: (0, 0)),
                  pl.BlockSpec((None, 1, d), lambda i: (i // bpb, 0, 0)),
                  pl.BlockSpec((None, 1, d), lambda i: (i // bpb, 0, 0))],
        out_specs=pl.BlockSpec((tm, d), lambda i: (i, 0)),
        out_shape=jax.ShapeDtypeStruct((t, d), BF16),
        compiler_params=_params("arbitrary"),
        name="norm_mod",
    )(x2d, g, sc, sh)


def _mm_act_kernel(a_ref, wt_ref, o_ref, *, act):
    acc = _dot_nt(a_ref[...], wt_ref[...].astype(BF16))
    if act == "gelu":
        acc = jax.nn.gelu(acc)
    elif act == "sigmoid":
        acc = _sigmoid(acc)
    o_ref[...] = acc.astype(o_ref.dtype)


def _matmul_act(a, w_t, tm, tn, row0, n_rows, act, name):
    m, k = a.shape
    assert n_rows % tn == 0 and row0 % 8 == 0
    return pl.pallas_call(
        functools.partial(_mm_act_kernel, act=act),
        grid=(m // tm, n_rows // tn),
        in_specs=[pl.BlockSpec((tm, k), lambda i, j: (i, 0)),
                  pl.BlockSpec((pl.Element(tn), pl.Element(k)),
                               lambda i, j: (pl.multiple_of(row0 + j * tn, 8), 0))],
        out_specs=pl.BlockSpec((tm, tn), lambda i, j: (i, j)),
        out_shape=jax.ShapeDtypeStruct((m, n_rows), BF16),
        compiler_params=_params("arbitrary", "arbitrary"),
        name=name,
    )(a, w_t)


def _alpha_kernel(h_ref, wt_ref, wup_ref, b_ref, o_ref, *, rank):
    a = _dot_nt(h_ref[...], wt_ref[...].astype(BF16))
    lane = lax.broadcasted_iota(I32, a.shape, 1)
    a = jnp.where(lane < rank, a, 0.0)
    z = _dot(a.astype(BF16), wup_ref[...]) + b_ref[...]
    log_sig = jnp.minimum(z, 0.0) - jnp.log1p(jnp.exp(-jnp.abs(z)))
    o_ref[...] = log_sig / GLA_TAU


def _alpha(h, w_t, row0, rank, wup_pad, b_alpha):
    t, d = h.shape
    key = wup_pad.shape[1]
    tm = _tile(t, 1024)
    assert row0 % 8 == 0 and row0 + LANES <= w_t.shape[0]
    return pl.pallas_call(
        functools.partial(_alpha_kernel, rank=rank),
        grid=(t // tm,),
        in_specs=[pl.BlockSpec((tm, d), lambda i: (i, 0)),
                  pl.BlockSpec((pl.Element(LANES), pl.Element(d)), lambda i: (row0, 0)),
                  pl.BlockSpec((LANES, key), lambda i: (0, 0)),
                  pl.BlockSpec((1, key), lambda i: (0, 0))],
        out_specs=pl.BlockSpec((tm, key), lambda i: (i, 0)),
        out_shape=jax.ShapeDtypeStruct((t, key), F32),
        compiler_params=_params("arbitrary"),
        name="gla_log_alpha",
    )(h, w_t, wup_pad, b_alpha)


def _gla_kernel(q_ref, k_ref, v_ref, la_ref, g_ref, gn_ref, o_ref, st_ref, *, n_chunks, scale):
    @pl.when(pl.program_id(1) == 0)
    def _():
        st_ref[...] = jnp.zeros_like(st_ref)

    c_len = GLA_CHUNK
    blk = n_chunks * c_len
    dk = q_ref.shape[1] // GLA_HEADS
    dv = v_ref.shape[1] // GLA_HEADS
    row = lax.broadcasted_iota(I32, (blk, blk), 0)
    col = lax.broadcasted_iota(I32, (blk, blk), 1)
    causal = col <= row
    tri = jnp.where(causal, 1.0, 0.0).astype(BF16)
    la = la_ref[...]
    la_hi = la.astype(BF16)
    la_lo = (la - la_hi.astype(F32)).astype(BF16)
    cum_all = _dot(tri, la_hi) + _dot(tri, la_lo)
    for hd in range(GLA_HEADS):
        ks = slice(hd * dk, (hd + 1) * dk)
        vs = slice(hd * dv, (hd + 1) * dv)
        cum = cum_all[:, ks]
        ends = [cum[(c + 1) * c_len - 1:(c + 1) * c_len, :] for c in range(n_chunks)]
        own_end = jnp.concatenate([jnp.broadcast_to(e, (c_len, dk)) for e in ends], axis=0)
        q = q_ref[:, ks].astype(F32) * scale
        k = k_ref[:, ks].astype(F32)
        v = v_ref[:, vs]
        k_rel = (k * jnp.exp(own_end - cum)).astype(BF16)
        cols = []
        for c in range(n_chunks):
            lo = c * c_len
            q_c = (q[lo:] * jnp.exp(cum[lo:] - ends[c])).astype(BF16)
            s_c = _dot_nt(q_c, k_rel[lo:lo + c_len, :])
            cols.append(s_c if c == 0 else jnp.concatenate([jnp.zeros((lo, c_len), F32), s_c], axis=0))
        scores = jnp.where(causal, jnp.concatenate(cols, axis=1), 0.0).astype(BF16)
        st = st_ref[hd]
        q_abs = (q * jnp.exp(cum)).astype(BF16)
        o = _dot(scores, v) + _dot_nt(q_abs, st.astype(BF16))
        k_end = (k * jnp.exp(ends[-1] - cum)).astype(BF16)
        st_ref[hd] = st * jnp.exp(ends[-1]) + _dot_tn(v, k_end)
        inv = lax.rsqrt(jnp.mean(o * o, axis=-1, keepdims=True) + EPS)
        g = g_ref[:, vs].astype(F32)
        o_ref[:, vs] = (o * inv * gn_ref[...] * (g * _sigmoid(g))).astype(o_ref.dtype)


def _gla(qkvg, log_a, gn, batch, seq, key, val):
    t = qkvg.shape[0]
    dk, dv = key // GLA_HEADS, val // GLA_HEADS
    blk = _tile(seq, 256)
    assert blk % GLA_CHUNK == 0 and (2 * key) % val == 0
    nb = seq // blk
    row = lambda b, n: b * nb + n
    return pl.pallas_call(
        functools.partial(_gla_kernel, n_chunks=blk // GLA_CHUNK, scale=float(dk) ** -0.5),
        grid=(batch, nb),
        in_specs=[pl.BlockSpec((blk, key), lambda b, n: (row(b, n), 0)),
                  pl.BlockSpec((blk, key), lambda b, n: (row(b, n), 1)),
                  pl.BlockSpec((blk, val), lambda b, n: (row(b, n), 2 * key // val)),
                  pl.BlockSpec((blk, key), lambda b, n: (row(b, n), 0)),
                  pl.BlockSpec((blk, val), lambda b, n: (row(b, n), 2 * key // val + 1)),
                  pl.BlockSpec((1, dv), lambda b, n: (0, 0))],
        out_specs=pl.BlockSpec((blk, val), lambda b, n: (row(b, n), 0)),
        out_shape=jax.ShapeDtypeStruct((t, val), BF16),
        scratch_shapes=[pltpu.VMEM((GLA_HEADS, dv, dk), F32)],
        compiler_params=_params("arbitrary", "arbitrary"),
        name="gla_mixer",
    )(qkvg, qkvg, qkvg, log_a, qkvg, gn)


def _sgu_kernel(u_ref, z_ref, lg_ref, lb_ref, w_ref, bs_ref, o_ref, *, n_chunks, groups):
    c_len = SGU_CHUNK
    gd = u_ref.shape[1] // groups
    row = lax.broadcasted_iota(I32, (c_len, c_len), 0)
    col = lax.broadcasted_iota(I32, (c_len, c_len), 1)
    causal = col <= row
    w_masked = [jnp.where(causal, w_ref[g], 0.0).astype(BF16) for g in range(groups)]
    for c in range(n_chunks):
        sl = slice(c * c_len, (c + 1) * c_len)
        z = z_ref[sl, :].astype(F32)
        mu = jnp.mean(z, axis=-1, keepdims=True)
        zc = z - mu
        var = jnp.mean(zc * zc, axis=-1, keepdims=True)
        zn = (zc * lax.rsqrt(var + EPS) * lg_ref[...] + lb_ref[...]).astype(BF16)
        for g in range(groups):
            gs = slice(g * gd, (g + 1) * gd)
            s = _dot(w_masked[g], zn[:, gs]) + bs_ref[:, gs]
            o_ref[sl, gs] = (u_ref[sl, gs].astype(F32) * s).astype(o_ref.dtype)


def _sgu(uz, ln_g, ln_b, w_spatial, bs_full, seq, width):
    t = uz.shape[0]
    groups = w_spatial.shape[0]
    blk = _tile(seq, 512)
    assert blk % SGU_CHUNK == 0
    return pl.pallas_call(
        functools.partial(_sgu_kernel, n_chunks=blk // SGU_CHUNK, groups=groups),
        grid=(t // blk,),
        in_specs=[pl.BlockSpec((blk, width), lambda i: (i, 0)),
                  pl.BlockSpec((blk, width), lambda i: (i, 1)),
                  pl.BlockSpec((1, width), lambda i: (0, 0)),
                  pl.BlockSpec((1, width), lambda i: (0, 0)),
                  pl.BlockSpec((groups, SGU_CHUNK, SGU_CHUNK), lambda i: (0, 0, 0)),
                  pl.BlockSpec((SGU_CHUNK, width), lambda i: (0, 0))],
        out_specs=pl.BlockSpec((blk, width), lambda i: (i, 0)),
        out_shape=jax.ShapeDtypeStruct((t, width), BF16),
        compiler_params=_params("arbitrary"),
        name="sgu_mixer",
    )(uz, uz, ln_g, ln_b, w_spatial, bs_full)


def _merge_kernel(a_ref, b_ref, wa_ref, wb_ref, ga_ref, gb_ref, o_ref):
    ya = _dot(a_ref[...], wa_ref[...].astype(BF16))
    yb = _dot(b_ref[...], wb_ref[...].astype(BF16))
    o_ref[...] = (ga_ref[...].astype(F32) * ya + gb_ref[...].astype(F32) * yb).astype(o_ref.dtype)


def _merge(a, b, wa, wb, gates, tm, tn):
    t, val = a.shape
    width = b.shape[1]
    d = wa.shape[1]
    return pl.pallas_call(
        _merge_kernel,
        grid=(t // tm, d // tn),
        in_specs=[pl.BlockSpec((tm, val), lambda i, j: (i, 0)),
                  pl.BlockSpec((tm, width), lambda i, j: (i, 0)),
                  pl.BlockSpec((val, tn), lambda i, j: (0, j)),
                  pl.BlockSpec((width, tn), lambda i, j: (0, j)),
                  pl.BlockSpec((tm, tn), lambda i, j: (i, j)),
                  pl.BlockSpec((tm, tn), lambda i, j: (i, d // tn + j))],
        out_specs=pl.BlockSpec((tm, tn), lambda i, j: (i, j)),
        out_shape=jax.ShapeDtypeStruct((t, d), BF16),
        compiler_params=_params("arbitrary", "arbitrary"),
        name="merge_branches",
    )(a, b, wa, wb, gates, gates)


def _outproj_kernel(m_ref, w_ref, x_ref, g1_ref, o_ref):
    y = _dot(m_ref[...], w_ref[...].astype(BF16))
    o_ref[...] = x_ref[...] + g1_ref[...] * y


def _outproj(merged, w, x2d, g1, seq, tm, tn):
    t, d = x2d.shape
    bpb = seq // tm
    return pl.pallas_call(
        _outproj_kernel,
        grid=(t // tm, d // tn),
        in_specs=[pl.BlockSpec((tm, d), lambda i, j: (i, 0)),
                  pl.BlockSpec((d, tn), lambda i, j: (0, j)),
                  pl.BlockSpec((tm, tn), lambda i, j: (i, j)),
                  pl.BlockSpec((None, 1, tn), lambda i, j: (i // bpb, 0, j))],
        out_specs=pl.BlockSpec((tm, tn), lambda i, j: (i, j)),
        out_shape=jax.ShapeDtypeStruct((t, d), F32),
        compiler_params=_params("arbitrary", "arbitrary"),
        name="out_proj_residual",
    )(merged, w, x2d, g1)


def _router_kernel(x_ref, g_ref, sc_ref, sh_ref, wr_ref, br_ref,
                   hp_ref, idx_ref, wt_ref, rank_ref, cnt_ref, carry_ref, *, n_exp):
    @pl.when(pl.program_id(0) == 0)
    def _():
        carry_ref[...] = jnp.zeros_like(carry_ref)

    x = x_ref[...]
    tm = x.shape[0]
    inv = lax.rsqrt(jnp.mean(x * x, axis=-1, keepdims=True) + EPS)
    h = (x * inv) * g_ref[...] * (1.0 + sc_ref[...]) + sh_ref[...]
    hp_ref[...] = _pack_halves(h)
    h_hi = h.astype(BF16)
    h_lo = (h - h_hi.astype(F32)).astype(BF16)
    def router_dot(a):
        pair = jnp.concatenate([a[:tm // 2], a[tm // 2:]], axis=1)
        out = _dot(pair, wr_ref[...])
        return jnp.concatenate([out[:, :LANES], out[:, LANES:]], axis=0)

    p = router_dot(h_hi)
    q = router_dot(h_lo)
    lane = lax.broadcasted_iota(I32, (tm, LANES), 1)
    logits = p + pltpu.roll(p, LANES - n_exp, 1) + q + br_ref[...]
    neg_inf = jnp.float32(-jnp.inf)
    work = jnp.where(lane < n_exp, logits, neg_inf)
    vals, sels = [], []
    idx_out = jnp.zeros((tm, LANES), I32)
    for k in range(TOP_K):
        m = jnp.max(work, axis=-1, keepdims=True)
        idx = jnp.min(jnp.where(work == m, lane, LANES), axis=-1, keepdims=True)
        sel = lane == idx
        work = jnp.where(sel, neg_inf, work)
        vals.append(m)
        sels.append(sel)
        idx_out = jnp.where(lane == k, idx, idx_out)
    exps = [jnp.exp(v - vals[0]) for v in vals]
    denom = exps[0]
    for e in exps[1:]:
        denom = denom + e
    wt_out = jnp.zeros((tm, LANES), F32)
    for k in range(TOP_K):
        wt_out = jnp.where(lane == k, exps[k] / denom, wt_out)
    onehot = jnp.zeros((tm, LANES), F32)
    for sel in sels:
        onehot = jnp.where(sel, 1.0, onehot)
    r = lax.broadcasted_iota(I32, (tm, tm), 0)
    c = lax.broadcasted_iota(I32, (tm, tm), 1)
    strict = jnp.where(c < r, 1.0, 0.0).astype(BF16)
    before = carry_ref[...] + _dot(strict, onehot.astype(BF16))
    rank_out = jnp.zeros((tm, LANES), I32)
    for k in range(TOP_K):
        rk = jnp.sum(jnp.where(sels[k], before, 0.0), axis=-1, keepdims=True)
        rank_out = jnp.where(lane == k, rk.astype(I32), rank_out)
    carry_ref[...] = carry_ref[...] + jnp.sum(onehot, axis=0, keepdims=True)
    idx_ref[...] = idx_out
    wt_ref[...] = wt_out
    rank_ref[...] = rank_out
    cnt_ref[...] = carry_ref[...]


def _router(x1, g, sc, sh, wr_cat, br_pad, seq, n_exp):
    t, d = x1.shape
    tm = _tile(seq, 256)
    bpb = seq // tm
    tok = pl.BlockSpec((tm, LANES), lambda i: (i, 0))
    return pl.pallas_call(
        functools.partial(_router_kernel, n_exp=n_exp),
        grid=(t // tm,),
        in_specs=[pl.BlockSpec((tm, d), lambda i: (i, 0)),
                  pl.BlockSpec((1, d), lambda i: (0, 0)),
                  pl.BlockSpec((None, 1, d), lambda i: (i // bpb, 0, 0)),
                  pl.BlockSpec((None, 1, d), lambda i: (i // bpb, 0, 0)),
                  pl.BlockSpec((2 * d, 2 * LANES), lambda i: (0, 0)),
                  pl.BlockSpec((1, LANES), lambda i: (0, 0))],
        out_specs=[pl.BlockSpec((tm, d // 2), lambda i: (i, 0)), tok, tok, tok,
                   pl.BlockSpec((1, LANES), lambda i: (0, 0))],
        out_shape=[jax.ShapeDtypeStruct((t, d // 2), U32),
                   jax.ShapeDtypeStruct((t, LANES), I32),
                   jax.ShapeDtypeStruct((t, LANES), F32),
                   jax.ShapeDtypeStruct((t, LANES), I32),
                   jax.ShapeDtypeStruct((1, LANES), F32)],
        scratch_shapes=[pltpu.VMEM((1, LANES), F32)],
        compiler_params=_params("arbitrary"),
        name="norm_router_topk",
    )(x1, g, sc, sh, wr_cat, br_pad)


def _dispatch_kernel(pad_end_ref, padded_ref, pstart_ref, idx_ref, rank_ref, hp_ref, xs_ref, zero_ref, zsem, sem,
                     *, tm, n_exp):
    @pl.when(pl.program_id(0) == 0)
    def _():
        zero_ref[...] = jnp.zeros_like(zero_ref)

        def zero_block(b):
            start = pl.multiple_of(b * MOE_ROWS, MOE_ROWS)
            return pltpu.make_async_copy(zero_ref, xs_ref.at[pl.ds(start, MOE_ROWS)], zsem)

        def start_tail(b, carry):
            zero_block(b).start()
            return carry

        def wait_tail(b, carry):
            zero_block(b).wait()
            return carry

        n_used = pad_end_ref[n_exp - 1] // MOE_ROWS
        n_blocks = xs_ref.shape[0] // MOE_ROWS
        for e in range(n_exp):
            @pl.when(padded_ref[e] > 0)
            def _():
                zero_block(pad_end_ref[e] // MOE_ROWS - 1).start()
        lax.fori_loop(n_used, n_blocks, start_tail, 0)
        for e in range(n_exp):
            @pl.when(padded_ref[e] > 0)
            def _():
                zero_block(pad_end_ref[e] // MOE_ROWS - 1).wait()
        lax.fori_loop(n_used, n_blocks, wait_tail, 0)

    def row_copy(r, p):
        return pltpu.make_async_copy(hp_ref.at[pl.ds(r, 1)], xs_ref.at[pl.ds(p, 1)], sem)

    def start(r, carry):
        for k in range(TOP_K):
            j = r * TOP_K + k
            row_copy(r, pstart_ref[idx_ref[j]] + rank_ref[j]).start(priority=k % 2)
        return carry

    lax.fori_loop(0, tm, start, 0)
    for _ in range(TOP_K):
        pltpu.make_async_copy(hp_ref, xs_ref.at[pl.ds(0, tm)], sem).wait()


def _dispatch(pad_end, padded, pad_start, idx_flat, rank_flat, hp, n_rows):
    t, half = hp.shape
    tm = _tile(t, 512)
    n_exp = pad_end.shape[0]
    grid_spec = pltpu.PrefetchScalarGridSpec(
        num_scalar_prefetch=3,
        grid=(t // tm,),
        in_specs=[pl.BlockSpec((tm * TOP_K,), lambda i, *_: (i,), memory_space=pltpu.SMEM),
                  pl.BlockSpec((tm * TOP_K,), lambda i, *_: (i,), memory_space=pltpu.SMEM),
                  pl.BlockSpec((tm, half), lambda i, *_: (i, 0))],
        out_specs=pl.BlockSpec(memory_space=pl.ANY),
        scratch_shapes=[pltpu.VMEM((MOE_ROWS, half), U32),
                        pltpu.SemaphoreType.DMA(()), pltpu.SemaphoreType.DMA(())],
    )
    return pl.pallas_call(
        functools.partial(_dispatch_kernel, tm=tm, n_exp=n_exp),
        grid_spec=grid_spec,
        out_shape=jax.ShapeDtypeStruct((n_rows, half), U32),
        compiler_params=_params("arbitrary"),
        name="moe_dispatch",
    )(pad_end, padded, pad_start, idx_flat, rank_flat, hp)


def _weight_runs(blk_exp, n_outer):
    n_blocks = blk_exp.shape[0]
    total = n_outer * n_blocks
    e_flat = jnp.tile(blk_exp, n_outer)
    j_flat = jnp.repeat(jnp.arange(n_outer, dtype=I32), n_blocks)
    changed = (e_flat[1:] != e_flat[:-1]) | (j_flat[1:] != j_flat[:-1])
    first = jnp.concatenate([jnp.ones((1,), bool), changed])
    slot = (jnp.cumsum(first.astype(I32)) - 1) % 2
    steps = jnp.arange(total, dtype=I32)
    nxt = lax.cummin(jnp.where(first, steps, total)[::-1])[::-1]
    nxt = jnp.concatenate([nxt[1:], jnp.full((1,), total, I32)])
    has_next = nxt < total
    nxt_c = jnp.minimum(nxt, total - 1)
    return (first.astype(I32), slot.astype(I32), has_next.astype(I32), e_flat[nxt_c].astype(I32),
            j_flat[nxt_c].astype(I32))


def _for_block_fill(fill, compute, out_ref):
    quarter = MOE_ROWS // 4
    for rows in range(quarter, MOE_ROWS + 1, quarter):
        @pl.when(jnp.logical_and(fill > rows - quarter, fill <= rows))
        def _(rows=rows):
            compute(rows)
            if rows < MOE_ROWS:
                out_ref[rows:, :] = jnp.zeros((MOE_ROWS - rows, out_ref.shape[1]), out_ref.dtype)

    @pl.when(fill == 0)
    def _():
        out_ref[...] = jnp.zeros_like(out_ref)


def _expert_up_kernel(be_ref, nv_ref, fill_ref, first_ref, slot_ref, pf_ok_ref, pf_e_ref, pf_j_ref,
                      x_ref, bg_ref, bu_ref, wg_hbm, wu_hbm, h_ref, stage_ref, sem, *, tf):
    del nv_ref
    j, i = pl.program_id(0), pl.program_id(1)
    s = j * pl.num_programs(1) + i
    slot = slot_ref[s]

    def w_copies(e, jj, sl):
        cols = pl.ds(pl.multiple_of(jj * tf, tf), tf)
        return [pltpu.make_async_copy(w.at[0, e, :, cols], stage_ref.at[sl, t], sem.at[sl, t])
                for t, w in enumerate((wg_hbm, wu_hbm))]

    @pl.when(s == 0)
    def _():
        for cp in w_copies(be_ref[0], 0, 0):
            cp.start()

    @pl.when(first_ref[s] == 1)
    def _():
        for cp in w_copies(be_ref[i], j, slot):
            cp.wait()

        @pl.when(pf_ok_ref[s] == 1)
        def _():
            for cp in w_copies(pf_e_ref[s], pf_j_ref[s], 1 - slot):
                cp.start()

    def compute(rows):
        xp = x_ref[:rows, :]
        half = xp.shape[1]
        x_lo = _unpack_lo(xp)
        x_hi = _unpack_hi(xp)
        wg_lo, wg_hi = stage_ref[slot, 0, :half, :], stage_ref[slot, 0, half:, :]
        wu_lo, wu_hi = stage_ref[slot, 1, :half, :], stage_ref[slot, 1, half:, :]
        glu = _dot(x_lo, wg_lo) + _dot(x_hi, wg_hi) + bg_ref[...]
        lin = _dot(x_lo, wu_lo) + _dot(x_hi, wu_hi) + bu_ref[...]
        glu = jnp.minimum(glu, SWIGLU_LIMIT)
        lin = jnp.clip(lin, -SWIGLU_LIMIT, SWIGLU_LIMIT)
        h_ref[:rows, :] = (glu * _sigmoid(SWIGLU_ALPHA * glu) * (lin + 1.0)).astype(h_ref.dtype)

    _for_block_fill(fill_ref[i], compute, h_ref)


def _expert_up(blk_exp, n_valid, blk_fill, xs, wg, bg, wu, bu, tf):
    n_rows, half = xs.shape
    d, f = wg.shape[-2], wg.shape[-1]
    n_blocks = n_rows // MOE_ROWS
    runs = _weight_runs(blk_exp, f // tf)
    last = lambda i, nv: jnp.minimum(i, nv[0] - 1)
    grid_spec = pltpu.PrefetchScalarGridSpec(
        num_scalar_prefetch=3 + len(runs),
        grid=(f // tf, n_blocks),
        in_specs=[pl.BlockSpec((MOE_ROWS, half), lambda j, i, be, nv, *_: (last(i, nv), 0)),
                  pl.BlockSpec((None, 1, tf), lambda j, i, be, nv, *_: (be[i], 0, j)),
                  pl.BlockSpec((None, 1, tf), lambda j, i, be, nv, *_: (be[i], 0, j)),
                  pl.BlockSpec(memory_space=pl.ANY),
                  pl.BlockSpec(memory_space=pl.ANY)],
        out_specs=pl.BlockSpec((MOE_ROWS, tf), lambda j, i, be, nv, *_: (i, j)),
        scratch_shapes=[pltpu.VMEM((2, 2, d, tf), F32), pltpu.SemaphoreType.DMA((2, 2))],
    )
    return pl.pallas_call(
        functools.partial(_expert_up_kernel, tf=tf),
        grid_spec=grid_spec,
        out_shape=jax.ShapeDtypeStruct((n_rows, f), F32),
        compiler_params=_params("arbitrary", "arbitrary"),
        name="expert_up",
    )(blk_exp, n_valid, blk_fill, *runs, xs, bg, bu, wg, wu)


def _expert_down_kernel(be_ref, nv_ref, fill_ref, first_ref, slot_ref, pf_ok_ref, pf_e_ref, pf_j_ref,
                        h_ref, bd_ref, wd_hbm, o_ref, stage_ref, sem, *, tn):
    del nv_ref
    j, i = pl.program_id(0), pl.program_id(1)
    s = j * pl.num_programs(1) + i
    slot = slot_ref[s]

    def w_copy(e, jj, sl):
        cols = pl.ds(pl.multiple_of(jj * tn, tn), tn)
        return pltpu.make_async_copy(wd_hbm.at[0, e, :, cols], stage_ref.at[sl], sem.at[sl])

    @pl.when(s == 0)
    def _():
        w_copy(be_ref[0], 0, 0).start()

    @pl.when(first_ref[s] == 1)
    def _():
        w_copy(be_ref[i], j, slot).wait()

        @pl.when(pf_ok_ref[s] == 1)
        def _():
            w_copy(pf_e_ref[s], pf_j_ref[s], 1 - slot).start()

    def compute(rows):
        y = _dot(h_ref[:rows, :], stage_ref[slot]) + bd_ref[...]
        o_ref[:rows, :] = _pack_halves(y)

    _for_block_fill(fill_ref[i], compute, o_ref)


def _expert_down(blk_exp, n_valid, blk_fill, hid, wd, bd, tn):
    n_rows, f = hid.shape
    d = wd.shape[-1]
    n_blocks = n_rows // MOE_ROWS
    runs = _weight_runs(blk_exp, d // tn)
    last = lambda i, nv: jnp.minimum(i, nv[0] - 1)
    grid_spec = pltpu.PrefetchScalarGridSpec(
        num_scalar_prefetch=3 + len(runs),
        grid=(d // tn, n_blocks),
        in_specs=[pl.BlockSpec((MOE_ROWS, f), lambda j, i, be, nv, *_: (last(i, nv), 0)),
                  pl.BlockSpec((None, 1, tn), lambda j, i, be, nv, *_: (be[i], 0, j)),
                  pl.BlockSpec(memory_space=pl.ANY)],
        out_specs=pl.BlockSpec((MOE_ROWS, tn // 2), lambda j, i, be, nv, *_: (i, j)),
        scratch_shapes=[pltpu.VMEM((2, f, tn), F32), pltpu.SemaphoreType.DMA((2,))],
    )
    return pl.pallas_call(
        functools.partial(_expert_down_kernel, tn=tn),
        grid_spec=grid_spec,
        out_shape=jax.ShapeDtypeStruct((n_rows, d // 2), U32),
        compiler_params=_params("arbitrary", "arbitrary"),
        name="expert_down",
    )(blk_exp, n_valid, blk_fill, *runs, hid, bd, wd)


def _combine_kernel(pstart_ref, idx_ref, rank_ref, idx_next_ref, rank_next_ref, wt_ref, x_ref, g2_ref, gf_ref,
                    ys_ref, o_ref, buf_ref, sem, *, tm, tn):
    i = pl.program_id(0)
    slot = i % 2

    def row_copy(sl, r, k, p):
        return pltpu.make_async_copy(ys_ref.at[pl.ds(p, 1)], buf_ref.at[sl, k, pl.ds(r, 1)], sem.at[sl])

    def issue(e_ref, r_ref, sl):
        def body(r, carry):
            for k in range(TOP_K):
                j = r * TOP_K + k
                row_copy(sl, r, k, pstart_ref[e_ref[j]] + r_ref[j]).start(priority=k % 2)
            return carry
        lax.fori_loop(0, tm, body, 0)

    @pl.when(i == 0)
    def _():
        issue(idx_ref, rank_ref, 0)

    @pl.when(i + 1 < pl.num_programs(0))
    def _():
        issue(idx_next_ref, rank_next_ref, 1 - slot)

    for k in range(TOP_K):
        pltpu.make_async_copy(ys_ref.at[pl.ds(0, tm)], buf_ref.at[slot, k], sem.at[slot]).wait()

    d = x_ref.shape[1]
    hw = tn // 2
    wts = [wt_ref[:, k:k + 1] for k in range(TOP_K)]
    pieces = []
    ssq = jnp.zeros((tm, 1), F32)
    for j in range(d // tn):
        for part, unpack in ((0, _unpack_lo), (1, _unpack_hi)):
            cols = slice(j * tn + part * hw, j * tn + (part + 1) * hw)
            y = jnp.zeros((tm, hw), F32)
            for k in range(TOP_K):
                y = y + wts[k] * unpack(buf_ref[slot, k, :, j * hw:(j + 1) * hw])
            x2 = x_ref[:, cols] + g2_ref[:, cols] * y
            ssq = ssq + jnp.sum(x2 * x2, axis=-1, keepdims=True)
            pieces.append((cols, x2))
    inv = lax.rsqrt(ssq / d + EPS)
    for cols, x2 in pieces:
        o_ref[:, cols] = x2 * inv * gf_ref[:, cols]


def _combine(pad_start, idx_flat, rank_flat, wt, x1, g2, gf, ys, seq, tn):
    t, d = x1.shape
    tm = _tile(seq, 256)
    bpb = seq // tm
    n_steps = t // tm
    cur = pl.BlockSpec((tm * TOP_K,), lambda i, ps: (i,), memory_space=pltpu.SMEM)
    nxt = pl.BlockSpec((tm * TOP_K,), lambda i, ps: (jnp.minimum(i + 1, n_steps - 1),), memory_space=pltpu.SMEM)
    grid_spec = pltpu.PrefetchScalarGridSpec(
        num_scalar_prefetch=1,
        grid=(n_steps,),
        in_specs=[cur, cur, nxt, nxt,
                  pl.BlockSpec((tm, LANES), lambda i, ps: (i, 0)),
                  pl.BlockSpec((tm, d), lambda i, ps: (i, 0)),
                  pl.BlockSpec((None, 1, d), lambda i, ps: (i // bpb, 0, 0)),
                  pl.BlockSpec((1, d), lambda i, ps: (0, 0)),
                  pl.BlockSpec(memory_space=pl.ANY)],
        out_specs=pl.BlockSpec((tm, d), lambda i, ps: (i, 0)),
        scratch_shapes=[pltpu.VMEM((2, TOP_K, tm, d // 2), U32), pltpu.SemaphoreType.DMA((2,))],
    )
    return pl.pallas_call(
        functools.partial(_combine_kernel, tm=tm, tn=tn),
        grid_spec=grid_spec,
        out_shape=jax.ShapeDtypeStruct((t, d), F32),
        compiler_params=_params("arbitrary"),
        name="moe_combine_final_norm",
    )(pad_start, idx_flat, rank_flat, idx_flat, rank_flat, wt, x1, g2, gf, ys)


def _layer(x2d, c_pad, batch, seq, w_ada, b_ada, norm_mix_g, w_in, w_alpha_up, b_alpha, gla_norm_g, sgu_ln_g,
           sgu_ln_b, w_spatial, b_spatial, w_branch_a, w_branch_b, w_out, norm_ffn_g, w_router, b_router,
           w_exp_gate, b_exp_gate, w_exp_up, b_exp_up, w_exp_down, b_exp_down, final_g):
    t, d = x2d.shape
    rank, key = w_alpha_up.shape
    val = w_branch_a.shape[0]
    width = w_branch_b.shape[0]
    n_exp = w_router.shape[1]
    f = w_exp_gate.shape[-1]
    assert 2 * n_exp <= LANES and rank <= LANES

    mod = _ada(c_pad, w_ada, b_ada.reshape(1, -1))[:batch].reshape(batch, 6, 1, d)
    sh1, sc1, g1, sh2, sc2, g2 = (mod[:, i] for i in range(6))

    h = _norm_mod(x2d, norm_mix_g.reshape(1, d), sc1, sh1, seq)

    tm = _tile(t, 1024)
    tn_dense = 512
    n_qkvg = 2 * key + 2 * val
    w_in_t = w_in.T
    qkvg = _matmul_act(h, w_in_t, tm, tn_dense, 0, n_qkvg, None, "proj_qkvg")
    wup = jnp.pad(w_alpha_up, ((0, LANES - rank), (0, 0))).astype(BF16)
    log_a = _alpha(h, w_in_t, n_qkvg, rank, wup, b_alpha.reshape(1, key))
    uz = _matmul_act(h, w_in_t, tm, tn_dense, n_qkvg + rank, 2 * width, "gelu", "proj_uz")
    gates = _matmul_act(h, w_in_t, tm, tn_dense, n_qkvg + rank + 2 * width, 2 * d, "sigmoid", "proj_gates")

    a = _gla(qkvg, log_a, gla_norm_g.reshape(1, -1), batch, seq, key, val)
    bs_full = jnp.repeat(b_spatial.T, width // w_spatial.shape[0], axis=1)
    b = _sgu(uz, sgu_ln_g.reshape(1, width), sgu_ln_b.reshape(1, width), w_spatial, bs_full, seq, width)

    merged = _merge(a, b, w_branch_a, w_branch_b, gates, _tile(t, 2048), 256)
    x1 = _outproj(merged, w_out, x2d, g1, seq, _tile(seq, 1024), _tile(d, 512))

    w_hi = w_router.astype(BF16)
    w_lo = (w_router - w_hi.astype(F32)).astype(BF16)
    wr_cat = jnp.pad(jnp.concatenate([w_hi, w_lo], axis=1), ((0, 0), (0, LANES - 2 * n_exp)))
    zeros = jnp.zeros_like(wr_cat)
    wr_pair = jnp.concatenate([jnp.concatenate([wr_cat, zeros], axis=1),
                               jnp.concatenate([zeros, wr_cat], axis=1)], axis=0)
    br_pad = jnp.pad(b_router.reshape(1, n_exp), ((0, 0), (0, LANES - n_exp)))
    hp, top_idx, top_w, rank_in_exp, counts = _router(x1, norm_ffn_g.reshape(1, d), sc2, sh2, wr_pair, br_pad,
                                                      seq, n_exp)

    counts = counts[0, :n_exp].astype(I32)
    padded = (counts + MOE_ROWS - 1) // MOE_ROWS * MOE_ROWS
    pad_end = jnp.cumsum(padded)
    pad_start = pad_end - padded
    idx_flat = top_idx[:, :TOP_K].reshape(-1)
    rank_flat = rank_in_exp[:, :TOP_K].reshape(-1)
    pad_start = pad_start.astype(I32)
    n_rows = t * TOP_K + n_exp * MOE_ROWS
    n_blocks = n_rows // MOE_ROWS
    blk_start = jnp.arange(n_blocks, dtype=I32) * MOE_ROWS
    blk_exp = jnp.minimum(jnp.sum(blk_start[:, None] >= pad_end[None, :], axis=1), n_exp - 1).astype(I32)
    n_valid = (pad_end[-1:] // MOE_ROWS).astype(I32)
    blk_fill = jnp.clip((pad_start + counts)[blk_exp] - blk_start, 0, MOE_ROWS).astype(I32)

    xs = _dispatch(pad_end.astype(I32), padded.astype(I32), pad_start, idx_flat, rank_flat, hp, n_rows)
    hid = _expert_up(blk_exp, n_valid, blk_fill, xs, w_exp_gate, b_exp_gate.reshape(n_exp, 1, f),
                     w_exp_up, b_exp_up.reshape(n_exp, 1, f), _tile(f, 512))
    tn_down = _tile(d, 4096)
    ys = _expert_down(blk_exp, n_valid, blk_fill, hid, w_exp_down, b_exp_down.reshape(n_exp, 1, d), tn_down)
    return _combine(pad_start, idx_flat, rank_flat, top_w, x1, g2, final_g, ys, seq, tn_down)


def kernel(x, c, w_ada, b_ada, norm_mix_g, w_in, w_alpha_up, b_alpha, gla_norm_g, sgu_ln_g, sgu_ln_b,
           w_spatial, b_spatial, w_branch_a, w_branch_b, w_out, norm_ffn_g, w_router, b_router,
           w_exp_gate, b_exp_gate, w_exp_up, b_exp_up, w_exp_down, b_exp_down, norm_final_g):
    batch, seq, d = x.shape
    depth = w_ada.shape[0]
    assert depth == 1, "the final rmsnorm is fused into the last layer's combine kernel"
    c_pad = jnp.pad(c, ((0, 8 - batch % 8), (0, 0))) if batch % 8 else c
    out = _layer(x.reshape(batch * seq, d), c_pad, batch, seq, w_ada[0], b_ada[0], norm_mix_g[0], w_in[0],
                 w_alpha_up[0], b_alpha[0], gla_norm_g[0], sgu_ln_g[0], sgu_ln_b[0], w_spatial[0],
                 b_spatial[0], w_branch_a[0], w_branch_b[0], w_out[0], norm_ffn_g[0], w_router[0],
                 b_router[0], w_exp_gate, b_exp_gate[0], w_exp_up, b_exp_up[0], w_exp_down, b_exp_down[0],
                 norm_final_g.reshape(1, d))
    return out.reshape(batch, seq, d)
```
